```python
import jax, jax.numpy as jnp
from jax import lax
import numpy as np

D_MODEL = 4096
BATCH = 16
SEQ = 256
DEPTH = 4
DEC_BATCH = 2
DEC_SEQ = 1024
PAST_LEN = 256

GRID_W = 64
N_EVEN = (DEPTH + 1) // 2
N_ODD = DEPTH // 2
N_MOD = 9
D_FF = 11008
EPS = 1e-6
POOL_WINDOWS = (2, 4, 8, 16)
POOL_GROUP = D_MODEL // 16
POOL_WIDTH = 4 * POOL_GROUP
QK_NOPE = 128
QK_ROPE = 64
V_HEAD = 128
Q_LORA = 768
KV_LORA = 512
MLA_HEADS = (D_MODEL - POOL_WIDTH) // V_HEAD
MLA_WIDTH = MLA_HEADS * V_HEAD
MLA_SCALE = (QK_NOPE + QK_ROPE) ** -0.5
EVEN_IN = POOL_WIDTH + Q_LORA + KV_LORA + QK_ROPE
AXIS_ROPE = QK_ROPE // 2
ROPE_BASE = 10000.0
Q_BLOCK = 128
REC_DK = 128
REC_DV = 128
REC_HEADS = D_MODEL // REC_DK
REC_WIDTH = REC_HEADS * REC_DK
CHUNK = 64

kernel_name = "hybrid_pool_mla_hgrn2_dit_step"


def rms_norm(x, gain=None):
    xf = x.astype(jnp.float32)
    y = xf * lax.rsqrt(jnp.mean(xf * xf, axis=-1, keepdims=True) + EPS)
    if gain is not None:
        y = y * gain.astype(jnp.float32)
    return y.astype(x.dtype)


def modulate(x, shift, scale):
    return rms_norm(x) * (1 + scale) + shift


def swiglu(h, w1, w3, w2):
    return (jax.nn.silu(h @ w1) * (h @ w3)) @ w2


def half_ffn(x, mod, base, w1, w3, w2):
    h = modulate(x, mod[..., base, :], mod[..., base + 1, :])
    return x + 0.5 * mod[..., base + 2, :] * swiglu(h, w1, w3, w2)


def axial_rope_tables(n_rows):
    t = jnp.arange(n_rows * GRID_W)
    row = (t // GRID_W).astype(jnp.float32)
    col = (t % GRID_W).astype(jnp.float32)
    inv = 1.0 / (ROPE_BASE ** (jnp.arange(0, AXIS_ROPE, 2, dtype=jnp.float32) / AXIS_ROPE))
    ang = jnp.stack([row[:, None] * inv, col[:, None] * inv], axis=1)
    return jnp.cos(ang), jnp.sin(ang)


def apply_axial_rope(x, cos, sin):
    xs = x.reshape(x.shape[:-1] + (2, 2, AXIS_ROPE // 2))
    x1, x2 = xs[..., 0, :], xs[..., 1, :]
    out = jnp.stack([x1 * cos - x2 * sin, x2 * cos + x1 * sin], axis=-2)
    return out.reshape(x.shape).astype(x.dtype)


def centred_pool_minus_self(a):
    B, L, _ = a.shape
    af = a.astype(jnp.float32)
    prefix = jnp.concatenate([jnp.zeros((B, 1, POOL_WIDTH), jnp.float32), jnp.cumsum(af, axis=1)], axis=1)
    t = jnp.arange(L)
    outs = []
    for g, w in enumerate(POOL_WINDOWS):
        lo = jnp.clip(t - w // 2, 0, L)
        hi = jnp.clip(t + w // 2, 0, L)
        p = prefix[..., g * POOL_GROUP:(g + 1) * POOL_GROUP]
        cnt = (hi - lo).astype(jnp.float32)[None, :, None]
        outs.append((p[:, hi] - p[:, lo]) / cnt)
    return (jnp.concatenate(outs, axis=-1) - af).astype(a.dtype)


def pool_mixer(a, w_pool, pool_scale):
    B, L, _ = a.shape
    p = centred_pool_minus_self(a).reshape(B, L, 4, POOL_GROUP)
    return jnp.einsum('blgc,gcd->blgd', p, w_pool).reshape(B, L, POOL_WIDTH) * pool_scale


def even_projections(h, w_in, q_a_gain, kv_a_gain, w_qb):
    B, L, _ = h.shape
    z = h @ w_in
    o1 = POOL_WIDTH
    o2 = o1 + Q_LORA
    o3 = o2 + KV_LORA
    a = z[..., :o1]
    cq = rms_norm(z[..., o1:o2], q_a_gain)
    ckv = rms_norm(z[..., o2:o3], kv_a_gain)
    kpe = z[..., o3:]
    q = (cq @ w_qb).reshape(B, L, MLA_HEADS, QK_NOPE + QK_ROPE)
    return a, q[..., :QK_NOPE], q[..., QK_NOPE:], ckv, kpe


def expand_kv(ckv, w_kvb):
    B, L, _ = ckv.shape
    kv = (ckv @ w_kvb).reshape(B, L, MLA_HEADS, QK_NOPE + V_HEAD)
    return kv[..., :QK_NOPE], kv[..., QK_NOPE:]


def mla_attention(q_nope, q_pe, k_nope, k_pe, v):
    B, Lq = q_nope.shape[:2]
    nb = Lq // Q_BLOCK

    def blocks(x):
        return jnp.moveaxis(x.reshape((B, nb, Q_BLOCK) + x.shape[2:]), 1, 0)

    def attend(qs):
        qn, qp = qs
        s = (jnp.einsum('bqhd,bkhd->bhqk', qn, k_nope)
             + jnp.einsum('bqhd,bkd->bhqk', qp, k_pe)).astype(jnp.float32) * MLA_SCALE
        p = jax.nn.softmax(s, axis=-1).astype(v.dtype)
        return jnp.einsum('bhqk,bkhd->bqhd', p, v)

    o = lax.map(attend, (blocks(q_nope), blocks(q_pe)))
    return jnp.moveaxis(o, 0, 1).reshape(B, Lq, MLA_WIDTH)


def even_mixer_context(h, w_in, q_a_gain, kv_a_gain, w_qb, w_kvb, w_pool, pool_scale, w_out):
    a, q_nope, q_pe, ckv, kpe = even_projections(h, w_in, q_a_gain, kv_a_gain, w_qb)
    k_nope, v = expand_kv(ckv, w_kvb)
    att = mla_attention(q_nope, q_pe, k_nope, kpe, v)
    y = jnp.concatenate([pool_mixer(a, w_pool, pool_scale), att], axis=-1) @ w_out
    return y, ckv, kpe


def even_mixer_latent(h, ckv_ctx, kpe_ctx, cos, sin, w_in, q_a_gain, kv_a_gain, w_qb, w_kvb, w_pool, pool_scale, w_out):
    a, q_nope, q_pe, ckv, kpe = even_projections(h, w_in, q_a_gain, kv_a_gain, w_qb)
    q_pe = apply_axial_rope(q_pe, cos[:, None], sin[:, None])
    kpe = apply_axial_rope(kpe, cos, sin)
    ckv_all = jnp.concatenate([ckv_ctx.astype(ckv.dtype), ckv], axis=1)
    kpe_all = jnp.concatenate([kpe_ctx.astype(kpe.dtype), kpe], axis=1)
    k_nope, v = expand_kv(ckv_all, w_kvb)
    att = mla_attention(q_nope, q_pe, k_nope, kpe_all, v)
    return jnp.concatenate([pool_mixer(a, w_pool, pool_scale), att], axis=-1) @ w_out


def layer_lower_bounds(lb_param):
    p = jax.nn.softmax(lb_param.astype(jnp.float32), axis=0)
    return jnp.cumsum(p, axis=0) - p[0]


def forget_gate(zf, lb):
    B, L, _ = zf.shape
    zf = zf.astype(jnp.float32)
    log_f = jnp.logaddexp(jnp.log(lb), jnp.log1p(-lb) + jax.nn.log_sigmoid(zf))
    k = (1.0 - lb) * jax.nn.sigmoid(-zf)
    return log_f.reshape(B, L, REC_HEADS, REC_DK), k.reshape(B, L, REC_HEADS, REC_DK)


def gla_chunked(q, k, v, log_f, h0):
    B, L, H, DK = q.shape
    DV = v.shape[-1]
    n = L // CHUNK

    def chunks(x):
        return x.astype(jnp.float32).reshape(B, n, CHUNK, H, x.shape[-1])

    qc, kc, vc, gc = chunks(q), chunks(k), chunks(v), chunks(log_f)
    b = jnp.cumsum(gc, axis=2)
    b_ref = b[:, :, CHUNK // 2 - 1][:, :, None]
    b_last = b[:, :, -1]
    tri = jnp.tril(jnp.ones((CHUNK, CHUNK), dtype=bool))
    scores = jnp.einsum('bnthk,bnshk->bnhts', qc * jnp.exp(b - b_ref), kc * jnp.exp(b_ref - b))
    scores = jnp.where(tri, scores, 0.0)
    o_intra = jnp.einsum('bnhts,bnshv->bnthv', scores, vc)
    q_in = qc * jnp.exp(b)
    k_out = kc * jnp.exp(b_last[:, :, None] - b)
    decay = jnp.exp(b_last)

    def step(S, xs):
        q_t, k_t, v_t, d_t = xs
        o = jnp.einsum('bthk,bhkv->bthv', q_t, S)
        S = S * d_t[..., None] + jnp.einsum('bthk,bthv->bhkv', k_t, v_t)
        return S, o

    xs = (jnp.moveaxis(q_in, 1, 0), jnp.moveaxis(k_out, 1, 0), jnp.moveaxis(vc, 1, 0), jnp.moveaxis(decay, 1, 0))
    S_final, o_inter = lax.scan(step, h0.astype(jnp.float32), xs)
    o = o_intra + jnp.moveaxis(o_inter, 0, 1)
    return o.reshape(B, L, H, DV).astype(v.dtype), S_final


def hgrn2_projections(h, w_in, lb_f, lb_b):
    B, L, _ = h.shape
    zq, zf, zb, zi, zg = jnp.split(h @ w_in, 5, axis=-1)
    q = jax.nn.silu(zq).reshape(B, L, REC_HEADS, REC_DK)
    i = zi.reshape(B, L, REC_HEADS, REC_DV)
    logf_f, k_f = forget_gate(zf, lb_f)
    logf_b, k_b = forget_gate(zb, lb_b)
    return q, i, k_f, logf_f, k_b, logf_b, zg


def bidir_hgrn2(q, i, k_f, logf_f, k_b, logf_b, s0_f, s0_b):
    o_f, s_f = gla_chunked(q, k_f, i, logf_f, s0_f)
    o_b, s_b = gla_chunked(q[:, ::-1], k_b[:, ::-1], i[:, ::-1], logf_b[:, ::-1], s0_b)
    return o_f + o_b[:, ::-1], s_f, s_b


def hgrn2_readout(o, zg, g_gain, w_out):
    B, L = o.shape[:2]
    return (rms_norm(o, g_gain).reshape(B, L, REC_WIDTH) * jax.nn.silu(zg)) @ w_out


def odd_mixer_context(h, lb_f, lb_b, w_in, g_gain, w_out):
    B = h.shape[0]
    q, i, k_f, logf_f, k_b, logf_b, zg = hgrn2_projections(h, w_in, lb_f, lb_b)
    zero = jnp.zeros((B, REC_HEADS, REC_DK, REC_DV), jnp.float32)
    o, s_f, s_b = bidir_hgrn2(q, i, k_f, logf_f, k_b, logf_b, zero, zero)
    return hgrn2_readout(o, zg, g_gain, w_out), jnp.stack([s_f, s_b], axis=1)


def odd_mixer_latent(h, state_ctx, lb_f, lb_b, w_in, g_gain, w_out):
    q, i, k_f, logf_f, k_b, logf_b, zg = hgrn2_projections(h, w_in, lb_f, lb_b)
    o, _, _ = bidir_hgrn2(q, i, k_f, logf_f, k_b, logf_b, state_ctx[:, 0], state_ctx[:, 1])
    return hgrn2_readout(o, zg, g_gain, w_out)


def setup_inputs(seed: int = 0) -> dict:
    key = jax.random.key(seed)
    ks = jax.random.split(key, 26)
    f32 = jnp.float32

    def nrm(k, shape, scale):
        return jax.random.normal(k, shape, f32) * scale

    def gain(k, shape):
        return 1.0 + 0.1 * jax.random.normal(k, shape, f32)

    return {
        "x_prompt": nrm(ks[0], (BATCH, SEQ, D_MODEL), 1.0),
        "x_sample": nrm(ks[1], (DEC_BATCH, DEC_SEQ, D_MODEL), 1.0),
        "cache_ckv": nrm(ks[2], (DEC_BATCH, N_EVEN, PAST_LEN, KV_LORA), 1.0),
        "cache_kpe": nrm(ks[3], (DEC_BATCH, N_EVEN, PAST_LEN, QK_ROPE), 1.0),
        "state_hgrn": nrm(ks[4], (DEC_BATCH, N_ODD, 2, REC_HEADS, REC_DK, REC_DV), 0.5),
        "c": nrm(ks[5], (DEC_BATCH, D_MODEL), 1.0),
        "c_ctx": nrm(ks[6], (D_MODEL,), 1.0),
        "w_ada": nrm(ks[7], (DEPTH, D_MODEL, N_MOD * D_MODEL), 0.5 * D_MODEL ** -0.5),
        "b_ada": nrm(ks[8], (DEPTH, N_MOD * D_MODEL), 0.02),
        "ffn_w1": nrm(ks[9], (DEPTH, 2, D_MODEL, D_FF), D_MODEL ** -0.5),
        "ffn_w3": nrm(ks[10], (DEPTH, 2, D_MODEL, D_FF), D_MODEL ** -0.5),
        "ffn_w2": nrm(ks[11], (DEPTH, 2, D_FF, D_MODEL), D_FF ** -0.5),
        "w_in_even": nrm(ks[12], (N_EVEN, D_MODEL, EVEN_IN), D_MODEL ** -0.5),
        "q_a_gain": gain(ks[13], (N_EVEN, Q_LORA)),
        "kv_a_gain": gain(ks[14], (N_EVEN, KV_LORA)),
        "w_qb": nrm(ks[15], (N_EVEN, Q_LORA, MLA_HEADS * (QK_NOPE + QK_ROPE)), Q_LORA ** -0.5),
        "w_kvb": nrm(ks[16], (N_EVEN, KV_LORA, MLA_HEADS * (QK_NOPE + V_HEAD)), KV_LORA ** -0.5),
        "w_pool": nrm(ks[17], (N_EVEN, 4, POOL_GROUP, POOL_GROUP), POOL_GROUP ** -0.5),
        "pool_scale": gain(ks[18], (N_EVEN, POOL_WIDTH)),
        "w_out_even": nrm(ks[19], (N_EVEN, POOL_WIDTH + MLA_WIDTH, D_MODEL), (POOL_WIDTH + MLA_WIDTH) ** -0.5),
        "w_in_odd": nrm(ks[20], (N_ODD, D_MODEL, 5 * REC_WIDTH), D_MODEL ** -0.5),
        "lb_fwd": nrm(ks[21], (DEPTH, REC_WIDTH), 0.1),
        "lb_bwd": nrm(ks[22], (DEPTH, REC_WIDTH), 0.1),
        "g_norm_gain": gain(ks[23], (N_ODD, REC_DV)),
        "w_out_odd": nrm(ks[24], (N_ODD, REC_WIDTH, D_MODEL), REC_WIDTH ** -0.5),
        "final_gain": gain(ks[25], (D_MODEL,)),
    }


def reference(x_prompt, x_sample, cache_ckv, cache_kpe, state_hgrn, c, c_ctx, w_ada, b_ada,
              ffn_w1, ffn_w3, ffn_w2, w_in_even, q_a_gain, kv_a_gain, w_qb, w_kvb, w_pool, pool_scale,
              w_out_even, w_in_odd, lb_fwd, lb_bwd, g_norm_gain, w_out_odd, final_gain):
    n_rows = x_sample.shape[1] // GRID_W
    cos, sin = axial_rope_tables(n_rows)
    lbf_all = layer_lower_bounds(lb_fwd)
    lbb_all = layer_lower_bounds(lb_bwd)
    xp, xs = x_prompt, x_sample
    new_ckv, new_kpe, new_states = [], [], []
    for layer in range(DEPTH):
        m_ctx = (jax.nn.silu(c_ctx) @ w_ada[layer] + b_ada[layer]).reshape(1, 1, N_MOD, D_MODEL)
        m_lat = (jax.nn.silu(c) @ w_ada[layer] + b_ada[layer]).reshape(-1, 1, N_MOD, D_MODEL)
        xp = half_ffn(xp, m_ctx, 0, ffn_w1[layer, 0], ffn_w3[layer, 0], ffn_w2[layer, 0])
        xs = half_ffn(xs, m_lat, 0, ffn_w1[layer, 0], ffn_w3[layer, 0], ffn_w2[layer, 0])
        hp = modulate(xp, m_ctx[..., 3, :], m_ctx[..., 4, :])
        hs = modulate(xs, m_lat[..., 3, :], m_lat[..., 4, :])
        if layer % 2 == 0:
            e = layer // 2
            wts = (w_in_even[e], q_a_gain[e], kv_a_gain[e], w_qb[e], w_kvb[e], w_pool[e], pool_scale[e], w_out_even[e])
            yp, ckv, kpe = even_mixer_context(hp, *wts)
            ys = even_mixer_latent(hs, cache_ckv[:, e], cache_kpe[:, e], cos, sin, *wts)
            new_ckv.append(ckv)
            new_kpe.append(kpe)
        else:
            o = layer // 2
            yp, st = odd_mixer_context(hp, lbf_all[layer], lbb_all[layer], w_in_odd[o], g_norm_gain[o], w_out_odd[o])
            ys = odd_mixer_latent(hs, state_hgrn[:, o], lbf_all[layer], lbb_all[layer], w_in_odd[o], g_norm_gain[o], w_out_odd[o])
            new_states.append(st)
        xp = xp + m_ctx[..., 5, :] * yp
        xs = xs + m_lat[..., 5, :] * ys
        xp = half_ffn(xp, m_ctx, 6, ffn_w1[layer, 1], ffn_w3[layer, 1], ffn_w2[layer, 1])
        xs = half_ffn(xs, m_lat, 6, ffn_w1[layer, 1], ffn_w3[layer, 1], ffn_w2[layer, 1])
    y_prompt = rms_norm(xp, final_gain)
    y_sample = rms_norm(xs, final_gain)
    new_ckv_arr = jnp.stack(new_ckv, axis=1)
    new_kpe_arr = jnp.stack(new_kpe, axis=1)
    new_hgrn = jnp.stack(new_states, axis=1)
    return (y_prompt, y_sample, new_ckv_arr, new_kpe_arr, new_hgrn)
```

```python
import functools

import jax
import jax.numpy as jnp
from jax import lax
from jax.experimental import pallas as pl
from jax.experimental.pallas import tpu as pltpu

F32 = jnp.float32
BF16 = jnp.bfloat16

EPS = 1e-6
N_MOD = 9
GRID_W = 64
ROPE_BASE = 10000.0
POOL_WINDOWS = (2, 4, 8, 16)
POOL_HALO = 8
QK_NOPE = 128
QK_ROPE = 64
V_HEAD = 128
REC_HEAD = 128
CHUNK = 64
LANES = 128
ROW_TILE = 256
MOD_ROWS = 8
MIB = 1 << 20


def _params(semantics, vmem_mib):
    return pltpu.CompilerParams(dimension_semantics=semantics, vmem_limit_bytes=vmem_mib * MIB)


class _Layout:
    def __init__(self, nb_lat, l_lat, nb_ctx, l_ctx):
        self.nb_lat, self.l_lat, self.nb_ctx, self.l_ctx = nb_lat, l_lat, nb_ctx, l_ctx
        self.t_lat = nb_lat * l_lat
        self.t_ctx = nb_ctx * l_ctx
        self.t = self.t_lat + self.t_ctx

    def group(self, i, rows):
        per = self.l_lat // rows
        return jnp.where(i < self.nb_lat * per, i // per, self.nb_lat)


def _fit(n, tile):
    tile = min(tile, n)
    while n % tile:
        tile //= 2
    return tile


def _silu(x):
    return x * jax.nn.sigmoid(x)


def _rms(x):
    return x * lax.rsqrt(jnp.mean(x * x, axis=-1, keepdims=True) + EPS)


def _ada_kernel(c_ref, w_ref, b_ref, o_ref):
    s = _silu(c_ref[...]).astype(BF16)
    o_ref[...] = jnp.dot(s, w_ref[...].astype(BF16), preferred_element_type=F32) + b_ref[...]


def _ada_mods(cmat, w_ada, b_ada, tn=512):
    depth, d, n = w_ada.shape
    return pl.pallas_call(
        _ada_kernel,
        grid=(depth, n // tn),
        in_specs=[pl.BlockSpec((MOD_ROWS, d), lambda l, j: (0, 0)),
                  pl.BlockSpec((None, d, tn), lambda l, j: (l, 0, j)),
                  pl.BlockSpec((None, 1, tn), lambda l, j: (l, 0, j))],
        out_specs=pl.BlockSpec((None, MOD_ROWS, tn), lambda l, j: (l, 0, j)),
        out_shape=jax.ShapeDtypeStruct((depth, MOD_ROWS, n), F32),
        compiler_params=_params(("arbitrary", "arbitrary"), 40),
        name="ada_mods",
    )(cmat, w_ada, b_ada.reshape(depth, 1, n))


def _trans_kernel(*refs, has_y, coef, gate_idx, mod_idx, final):
    it = iter(refs)
    x = next(it)[...]
    if has_y:
        y_ref, mg_ref = next(it), next(it)
        g = mg_ref[gate_idx:gate_idx + 1, :]
        if coef != 1.0:
            g = coef * g
        x = x + g * y_ref[...]
    xn = _rms(x)
    if final:
        gain_ref, o_ref = next(it), next(it)
        o_ref[...] = xn * gain_ref[...]
        return
    mm_ref = next(it)
    shift = mm_ref[mod_idx:mod_idx + 1, :]
    scale = mm_ref[mod_idx + 1:mod_idx + 2, :]
    if has_y:
        next(it)[...] = x
    next(it)[...] = (xn * (1.0 + scale) + shift).astype(BF16)


def _transition(lay, mods, x, y=None, gate=None, mod=None, final_gain=None, rows=None):
    t, d = x.shape
    tr = ROW_TILE
    off, n_tiles = rows if rows is not None else (0, t // tr)
    row_spec = pl.BlockSpec((tr, d), lambda i: (i + off, 0))

    def mod_spec(layer):
        return pl.BlockSpec((None, None, N_MOD, d), lambda i: (layer, lay.group(i + off, tr), 0, 0))

    args, in_specs = [x], [row_spec]
    has_y = y is not None
    coef, gate_idx, mod_idx = 1.0, 0, 0
    if has_y:
        gate_layer, gate_idx, coef = gate
        args += [y, mods]
        in_specs += [row_spec, mod_spec(gate_layer)]
    if final_gain is not None:
        args.append(final_gain.reshape(1, d))
        in_specs.append(pl.BlockSpec((1, d), lambda i: (0, 0)))
        out_shape = jax.ShapeDtypeStruct((n_tiles * tr, d), F32)
        out_specs = pl.BlockSpec((tr, d), lambda i: (i, 0))
    else:
        mod_layer, mod_idx = mod
        args.append(mods)
        in_specs.append(mod_spec(mod_layer))
        out_shape = [jax.ShapeDtypeStruct((t, d), BF16)]
        out_specs = [pl.BlockSpec((tr, d), lambda i: (i, 0))]
        if has_y:
            out_shape.insert(0, jax.ShapeDtypeStruct((t, d), F32))
            out_specs.insert(0, pl.BlockSpec((tr, d), lambda i: (i, 0)))
    out = pl.pallas_call(
        functools.partial(_trans_kernel, has_y=has_y, coef=coef, gate_idx=gate_idx, mod_idx=mod_idx,
                          final=final_gain is not None),
        grid=(n_tiles,),
        in_specs=in_specs, out_specs=out_specs, out_shape=out_shape,
        compiler_params=_params(("arbitrary",), 48),
        name="transition",
    )(*args)
    if final_gain is not None:
        return out
    return (out[0], out[1]) if has_y else (x, out[0])


def _ffn_kernel(h_ref, w1_ref, w3_ref, w2_ref, o_ref):
    @pl.when(pl.program_id(1) == 0)
    def _():
        o_ref[...] = jnp.zeros_like(o_ref)

    h = h_ref[...]
    a1 = jnp.dot(h, w1_ref[...].astype(BF16), preferred_element_type=F32)
    a3 = jnp.dot(h, w3_ref[...].astype(BF16), preferred_element_type=F32)
    g = (_silu(a1) * a3).astype(BF16)
    o_ref[...] += jnp.dot(g, w2_ref[...].astype(BF16), preferred_element_type=F32)


def _ffn(h, w1, w3, w2, layer, half, tm=1024, tf=256):
    t, d = h.shape
    dff = w1.shape[-1]
    tm = _fit(t, tm)
    return pl.pallas_call(
        _ffn_kernel,
        grid=(t // tm, dff // tf),
        in_specs=[pl.BlockSpec((tm, d), lambda i, j: (i, 0), pipeline_mode=pl.Buffered(1)),
                  pl.BlockSpec((None, None, d, tf), lambda i, j: (layer, half, 0, j)),
                  pl.BlockSpec((None, None, d, tf), lambda i, j: (layer, half, 0, j)),
                  pl.BlockSpec((None, None, tf, d), lambda i, j: (layer, half, j, 0))],
        out_specs=pl.BlockSpec((tm, d), lambda i, j: (i, 0), pipeline_mode=pl.Buffered(1)),
        out_shape=jax.ShapeDtypeStruct((t, d), F32),
        compiler_params=_params(("arbitrary", "arbitrary"), 60),
        name="ffn",
    )(h, w1, w3, w2)


def _mm_kernel(*refs, n_parts, has_gain):
    acc = None
    for x_ref, w_ref in zip(refs[:n_parts], refs[n_parts:2 * n_parts]):
        part = jnp.dot(x_ref[...].astype(BF16), w_ref[...].astype(BF16), preferred_element_type=F32)
        acc = part if acc is None else acc + part
    if has_gain:
        acc = _rms(acc) * refs[2 * n_parts][...]
    o_ref = refs[-1]
    o_ref[...] = acc.astype(o_ref.dtype)


def _mm(parts, w, n_out, *, lead=(), tm=1024, tn=512, out_dtype=F32, gain=None, vmem_mib=56):
    t = parts[0][0].shape[0]
    kp = w.shape[-2] // len(parts)
    tm, tn = _fit(t, tm), _fit(n_out, tn)
    if gain is not None:
        assert tn == n_out
    nl = (None,) * len(lead)
    in_specs = [pl.BlockSpec((tm, kp), functools.partial(lambda i, j, cb: (i, cb), cb=cb)) for _, cb in parts]
    in_specs += [pl.BlockSpec(nl + (kp, tn), functools.partial(lambda i, j, p: lead + (p, j), p=p))
                 for p in range(len(parts))]
    args = [a for a, _ in parts] + [w] * len(parts)
    if gain is not None:
        args.append(gain.reshape(1, n_out))
        in_specs.append(pl.BlockSpec((1, n_out), lambda i, j: (0, 0)))
    return pl.pallas_call(
        functools.partial(_mm_kernel, n_parts=len(parts), has_gain=gain is not None),
        grid=(t // tm, n_out // tn),
        in_specs=in_specs,
        out_specs=pl.BlockSpec((tm, tn), lambda i, j: (i, j)),
        out_shape=jax.ShapeDtypeStruct((t, n_out), out_dtype),
        compiler_params=_params(("arbitrary", "arbitrary"), vmem_mib),
        name="proj",
    )(*args)


def _pool_kernel(a_ref, prev_ref, next_ref, w_ref, s_ref, o_ref, pad_ref, *, lay):
    tr = ROW_TILE
    grp = a_ref.shape[1] // len(POOL_WINDOWS)
    i = pl.program_id(0)
    per = lay.l_lat // tr
    is_lat = i < lay.nb_lat * per
    k = i % per
    has_prev = jnp.logical_and(is_lat, k != 0)
    has_next = jnp.logical_and(is_lat, k != per - 1)
    pad_ref[0:POOL_HALO, :] = jnp.where(has_prev, prev_ref[...], 0.0)
    pad_ref[POOL_HALO:POOL_HALO + tr, :] = a_ref[...]
    pad_ref[POOL_HALO + tr:, :] = jnp.where(has_next, next_ref[...], 0.0)
    t = jnp.where(is_lat, k * tr, 0) + lax.broadcasted_iota(jnp.int32, (tr, 1), 0)
    seq_len = jnp.where(is_lat, lay.l_lat, lay.l_ctx)
    for g, win in enumerate(POOL_WINDOWS):
        cols = slice(g * grp, (g + 1) * grp)
        acc = None
        for dlt in range(-(win // 2), win // 2):
            v = pad_ref[POOL_HALO + dlt:POOL_HALO + dlt + tr, cols]
            acc = v if acc is None else acc + v
        cnt = (jnp.clip(t + win // 2, 0, seq_len) - jnp.clip(t - win // 2, 0, seq_len)).astype(F32)
        p = acc / cnt - a_ref[:, cols]
        y = jnp.dot(p.astype(BF16), w_ref[g].astype(BF16), preferred_element_type=F32) * s_ref[:, cols]
        o_ref[:, cols] = y.astype(o_ref.dtype)


def _pool_mixer(lay, a, w_pool, pool_scale, e):
    t, width = a.shape
    tr, hb = ROW_TILE, ROW_TILE // POOL_HALO
    n_halo = t // POOL_HALO
    grp = width // len(POOL_WINDOWS)
    return pl.pallas_call(
        functools.partial(_pool_kernel, lay=lay),
        grid=(t // tr,),
        in_specs=[pl.BlockSpec((tr, width), lambda i: (i, 0)),
                  pl.BlockSpec((POOL_HALO, width), lambda i: (jnp.maximum(i * hb - 1, 0), 0)),
                  pl.BlockSpec((POOL_HALO, width), lambda i: (jnp.minimum((i + 1) * hb, n_halo - 1), 0)),
                  pl.BlockSpec((None, len(POOL_WINDOWS), grp, grp), lambda i: (e, 0, 0, 0)),
                  pl.BlockSpec((None, 1, width), lambda i: (e, 0, 0))],
        out_specs=pl.BlockSpec((tr, width), lambda i: (i, 0)),
        out_shape=jax.ShapeDtypeStruct((t, width), BF16),
        scratch_shapes=[pltpu.VMEM((tr + 2 * POOL_HALO, width), F32)],
        compiler_params=_params(("arbitrary",), 32),
        name="pool_mixer",
    )(a, a, a, w_pool, pool_scale.reshape(pool_scale.shape[0], 1, width))


def _rope_swap(x):
    lane = lax.broadcasted_iota(jnp.int32, (1, x.shape[-1]), 1)
    half = QK_ROPE // 4
    return jnp.where(lane % (2 * half) < half, pltpu.roll(x, x.shape[-1] - half, 1), pltpu.roll(x, half, 1))


def _attn_kernel(*refs, rope, scale):
    qn_ref, qp_ref, kv_ref, kpe_ref = refs[:4]
    o_ref = refs[-1]
    qp = qp_ref[...]
    kpe = kpe_ref[...]
    if rope:
        cq_ref, sq_ref, ck_ref, sk_ref = refs[4:8]
        qp = qp * cq_ref[...] + _rope_swap(qp) * sq_ref[...]
        kpe = kpe * ck_ref[...] + _rope_swap(kpe) * sk_ref[...]
    kpe = kpe.astype(BF16)
    lane = lax.broadcasted_iota(jnp.int32, (1, 2 * QK_ROPE), 1)
    for hh in range(2):
        own = jnp.logical_and(lane >= hh * QK_ROPE, lane < (hh + 1) * QK_ROPE)
        q = jnp.concatenate([qn_ref[:, hh * QK_NOPE:(hh + 1) * QK_NOPE], jnp.where(own, qp, 0.0)], axis=-1)
        base = hh * (QK_NOPE + V_HEAD)
        k = jnp.concatenate([kv_ref[:, base:base + QK_NOPE], kpe], axis=-1)
        v = kv_ref[:, base + QK_NOPE:base + QK_NOPE + V_HEAD]
        s = lax.dot_general(q.astype(BF16), k, (((1,), (1,)), ((), ())), preferred_element_type=F32) * scale
        p = jnp.exp(s - jnp.max(s, axis=-1, keepdims=True))
        p = p / jnp.sum(p, axis=-1, keepdims=True)
        o = jnp.dot(p.astype(BF16), v, preferred_element_type=F32)
        o_ref[:, hh * V_HEAD:(hh + 1) * V_HEAD] = o.astype(o_ref.dtype)


def _attention(q, kv, kpe2, n_heads, *, n_seq, lq, lk, q_row0, kv_row0, tq, tables=None):
    n_pairs = n_heads // 2
    qt = lq // tq
    rope_col0 = n_heads * QK_NOPE // (2 * QK_ROPE)
    scale = float(QK_NOPE + QK_ROPE) ** -0.5

    def q_row(b, qi):
        return q_row0 // tq + b * qt + qi

    in_specs = [pl.BlockSpec((tq, 2 * QK_NOPE), lambda b, hp, qi: (q_row(b, qi), hp)),
                pl.BlockSpec((tq, 2 * QK_ROPE), lambda b, hp, qi: (q_row(b, qi), rope_col0 + hp)),
                pl.BlockSpec((lk, 2 * (QK_NOPE + V_HEAD)), lambda b, hp, qi: (kv_row0 // lk + b, hp)),
                pl.BlockSpec((lk, 2 * QK_ROPE), lambda b, hp, qi: (kv_row0 // lk + b, 0))]
    args = [q, q, kv, kpe2]
    if tables is not None:
        cq, sq, ck, sk = tables
        in_specs += [pl.BlockSpec((tq, 2 * QK_ROPE), lambda b, hp, qi: (qi, 0)),
                     pl.BlockSpec((tq, 2 * QK_ROPE), lambda b, hp, qi: (qi, 0)),
                     pl.BlockSpec((lk, 2 * QK_ROPE), lambda b, hp, qi: (0, 0)),
                     pl.BlockSpec((lk, 2 * QK_ROPE), lambda b, hp, qi: (0, 0))]
        args += [cq, sq, ck, sk]
    return pl.pallas_call(
        functools.partial(_attn_kernel, rope=tables is not None, scale=scale),
        grid=(n_seq, n_pairs, qt),
        in_specs=in_specs,
        out_specs=pl.BlockSpec((tq, 2 * V_HEAD), lambda b, hp, qi: (b * qt + qi, hp)),
        out_shape=jax.ShapeDtypeStruct((n_seq * lq, n_heads * V_HEAD), BF16),
        compiler_params=_params(("arbitrary", "arbitrary", "arbitrary"), 32),
        name="attention",
    )(*args)


def _rope_tables(l_lat, past):
    t = jnp.arange(l_lat)
    row = (t // GRID_W).astype(F32)
    col = (t % GRID_W).astype(F32)
    axis = QK_ROPE // 2
    inv = 1.0 / (ROPE_BASE ** (jnp.arange(0, axis, 2, dtype=F32) / axis))
    ar, ac = row[:, None] * inv, col[:, None] * inv
    cos = jnp.concatenate([jnp.cos(ar), jnp.cos(ar), jnp.cos(ac), jnp.cos(ac)], axis=-1)
    sin = jnp.concatenate([-jnp.sin(ar), jnp.sin(ar), -jnp.sin(ac), jnp.sin(ac)], axis=-1)
    cq, sq = jnp.tile(cos, (1, 2)), jnp.tile(sin, (1, 2))
    ck = jnp.concatenate([jnp.ones((past, 2 * QK_ROPE), F32), cq], axis=0)
    sk = jnp.concatenate([jnp.zeros((past, 2 * QK_ROPE), F32), sq], axis=0)
    return cq, sq, ck, sk


def _lower_bound(lb, layer):
    rows = [lb[i:i + 1, :] for i in range(lb.shape[0])]
    m = functools.reduce(jnp.maximum, rows)
    e = [jnp.exp(r - m) for r in rows]
    total = functools.reduce(lambda a, b: a + b, e)
    p = [x / total for x in e]
    cum = functools.reduce(lambda a, b: a + b, p[:layer + 1])
    return cum - p[0]


def _cumsum_rows(x, tri):
    hi = x.astype(BF16)
    r1 = x - hi.astype(F32)
    mid = r1.astype(BF16)
    lo = (r1 - mid.astype(F32)).astype(BF16)
    return (jnp.dot(tri, hi, preferred_element_type=F32) + jnp.dot(tri, mid, preferred_element_type=F32)
            + jnp.dot(tri, lo, preferred_element_type=F32))


def _gla_kernel(*refs, layer, n_chunks, hg, has_state_in, has_state_out):
    it = iter(refs)
    zq_ref, zf_ref, zb_ref, zi_ref, zg_ref, lbf_ref, lbb_ref, gain_ref = [next(it) for _ in range(8)]
    s_in_ref = next(it) if has_state_in else None
    r_ref = next(it)
    s_out_ref = next(it) if has_state_out else None
    of_ref, ob_ref, st_ref = next(it), next(it), next(it)

    row = lax.broadcasted_iota(jnp.int32, (CHUNK, CHUNK), 0)
    col = lax.broadcasted_iota(jnp.int32, (CHUNK, CHUNK), 1)
    masks = (col <= row, col >= row)
    tris = tuple(jnp.where(m, 1.0, 0.0).astype(BF16) for m in masks)
    gates = []
    for lb_ref in (lbf_ref, lbb_ref):
        lb = _lower_bound(lb_ref[...], layer)
        gates.append((jnp.log(lb), jnp.log1p(-lb), 1.0 - lb))

    for dr in range(2):
        for h in range(hg):
            st_ref[dr, h] = s_in_ref[dr, h].T if has_state_in else jnp.zeros((REC_HEAD, REC_HEAD), F32)

    def step(c, rev):
        r0 = c * CHUNK if isinstance(c, int) else pl.multiple_of(c * CHUNK, CHUNK)
        rows = pl.ds(r0, CHUNK)
        log_lb, log_1m_lb, one_m_lb = gates[rev]
        q = _silu(zq_ref[rows, :])
        z = (zb_ref if rev else zf_ref)[rows, :]
        log_sig = jnp.minimum(z, 0.0) - jnp.log1p(jnp.exp(-jnp.abs(z)))
        u = log_1m_lb + log_sig
        log_f = jnp.maximum(log_lb, u) + jnp.log1p(jnp.exp(-jnp.abs(log_lb - u)))
        k = one_m_lb * jax.nn.sigmoid(-z)
        v = zi_ref[rows, :].astype(BF16)
        b = _cumsum_rows(log_f, tris[rev])
        b_mid = b[CHUNK // 2:CHUNK // 2 + 1] if rev else b[CHUNK // 2 - 1:CHUNK // 2]
        b_last = b[0:1] if rev else b[CHUNK - 1:CHUNK]
        q_intra = (q * jnp.exp(b - b_mid)).astype(BF16)
        k_intra = (k * jnp.exp(b_mid - b)).astype(BF16)
        q_in = (q * jnp.exp(b)).astype(BF16)
        k_out = (k * jnp.exp(b_last - b)).astype(BF16)
        decay = jnp.exp(b_last)
        o_dst = ob_ref if rev else of_ref
        for h in range(hg):
            sl = slice(h * REC_HEAD, (h + 1) * REC_HEAD)
            sc = lax.dot_general(q_intra[:, sl], k_intra[:, sl], (((1,), (1,)), ((), ())),
                                 preferred_element_type=F32)
            sc = jnp.where(masks[rev], sc, 0.0).astype(BF16)
            st = st_ref[rev, h]
            o = (jnp.dot(sc, v[:, sl], preferred_element_type=F32)
                 + lax.dot_general(q_in[:, sl], st.astype(BF16), (((1,), (1,)), ((), ())),
                                   preferred_element_type=F32))
            st_ref[rev, h] = st * decay[:, sl] + lax.dot_general(
                v[:, sl], k_out[:, sl], (((0,), (0,)), ((), ())), preferred_element_type=F32)
            o_dst[rows, sl] = o

    if n_chunks <= 4:
        for c in range(n_chunks):
            step(c, 0)
            step(n_chunks - 1 - c, 1)
    else:
        def body(c, carry):
            step(c, 0)
            step(n_chunks - 1 - c, 1)
            return carry
        lax.fori_loop(0, n_chunks, body, 0)

    gate = _silu(zg_ref[...])
    o = of_ref[...] + ob_ref[...]
    for h in range(hg):
        sl = slice(h * REC_HEAD, (h + 1) * REC_HEAD)
        r_ref[:, sl] = (_rms(o[:, sl]) * gain_ref[...] * gate[:, sl]).astype(r_ref.dtype)
    if has_state_out:
        for dr in range(2):
            for h in range(hg):
                s_out_ref[dr, h] = st_ref[dr, h].T


def _gla(z, lb_fwd, lb_bwd, g_gain, layer, *, n_seq, seq_len, row0, state_in=None, odd=0, want_state=False,
         hg=4):
    width = z.shape[1] // 5
    n_heads = width // REC_HEAD
    w = hg * REC_HEAD
    ncb = width // w

    def z_spec(part):
        return pl.BlockSpec((seq_len, w), lambda s, g: (row0 // seq_len + s, part * ncb + g))

    in_specs = [z_spec(p) for p in range(5)]
    in_specs += [pl.BlockSpec((lb_fwd.shape[0], w), lambda s, g: (0, g)),
                 pl.BlockSpec((lb_bwd.shape[0], w), lambda s, g: (0, g)),
                 pl.BlockSpec((1, REC_HEAD), lambda s, g: (0, 0))]
    args = [z] * 5 + [lb_fwd, lb_bwd, g_gain.reshape(1, REC_HEAD)]
    if state_in is not None:
        in_specs.append(pl.BlockSpec((None, None, 2, hg, REC_HEAD, REC_HEAD), lambda s, g: (s, odd, 0, g, 0, 0)))
        args.append(state_in)
    out_shape = [jax.ShapeDtypeStruct((n_seq * seq_len, width), BF16)]
    out_specs = [pl.BlockSpec((seq_len, w), lambda s, g: (s, g))]
    if want_state:
        out_shape.append(jax.ShapeDtypeStruct((n_seq, 2, n_heads, REC_HEAD, REC_HEAD), F32))
        out_specs.append(pl.BlockSpec((None, 2, hg, REC_HEAD, REC_HEAD), lambda s, g: (s, 0, g, 0, 0)))
    return pl.pallas_call(
        functools.partial(_gla_kernel, layer=layer, n_chunks=seq_len // CHUNK, hg=hg,
                          has_state_in=state_in is not None, has_state_out=want_state),
        grid=(n_seq, ncb),
        in_specs=in_specs, out_specs=out_specs, out_shape=out_shape,
        scratch_shapes=[pltpu.VMEM((seq_len, w), F32), pltpu.VMEM((seq_len, w), F32),
                        pltpu.VMEM((2, hg, REC_HEAD, REC_HEAD), F32)],
        compiler_params=_params(("arbitrary", "arbitrary"), 48),
        name="hgrn2",
    )(*args)


def _even_mixer(lay, h, e, cache_ckv, cache_kpe, w_in_even, q_a_gain, kv_a_gain, w_qb, w_kvb, w_pool,
                pool_scale, w_out_even):
    pool_width = pool_scale.shape[1]
    q_lora, kv_lora = q_a_gain.shape[1], kv_a_gain.shape[1]
    n_heads = w_qb.shape[2] // (QK_NOPE + QK_ROPE)
    past = cache_ckv.shape[2]
    o1, o2, o3 = pool_width, pool_width + q_lora, pool_width + q_lora + kv_lora

    a = _mm([(h, 0)], w_in_even, pool_width, lead=(e,))
    cq = _mm([(h, 0)], w_in_even[e, :, o1:o2], q_lora, tn=q_lora, out_dtype=BF16, gain=q_a_gain[e])
    ckv = _mm([(h, 0)], w_in_even[e, :, o2:o3], kv_lora, tn=kv_lora, gain=kv_a_gain[e])
    kpe = _mm([(h, 0)], w_in_even[e, :, o3:], QK_ROPE)

    wq = w_qb[e].reshape(q_lora, n_heads, QK_NOPE + QK_ROPE)
    wq = jnp.concatenate([wq[:, :, :QK_NOPE].reshape(q_lora, -1), wq[:, :, QK_NOPE:].reshape(q_lora, -1)], axis=1)
    q = _mm([(cq, 0)], wq, wq.shape[1])

    def with_cache(cache, new):
        lat = jnp.concatenate([cache, new[:lay.t_lat].reshape(lay.nb_lat, lay.l_lat, -1)], axis=1)
        return jnp.concatenate([lat.reshape(lay.nb_lat * (past + lay.l_lat), -1), new[lay.t_lat:]], axis=0)

    ckv_all = with_cache(cache_ckv[:, e], ckv)
    kpe_all = jnp.tile(with_cache(cache_kpe[:, e], kpe), (1, 2))
    kv = _mm([(ckv_all, 0)], w_kvb, w_kvb.shape[2], lead=(e,), tm=512, out_dtype=BF16)

    lk_lat = past + lay.l_lat
    att_lat = _attention(q, kv, kpe_all, n_heads, n_seq=lay.nb_lat, lq=lay.l_lat, lk=lk_lat, q_row0=0,
                         kv_row0=0, tq=ROW_TILE, tables=_rope_tables(lay.l_lat, past))
    att_ctx = _attention(q, kv, kpe_all, n_heads, n_seq=lay.nb_ctx, lq=lay.l_ctx, lk=lay.l_ctx,
                         q_row0=lay.t_lat, kv_row0=lay.nb_lat * lk_lat, tq=lay.l_ctx)
    att = jnp.concatenate([att_lat, att_ctx], axis=0)
    pooled = _pool_mixer(lay, a, w_pool, pool_scale, e)

    parts = [(pooled, 0)] + [(att, cb) for cb in range(att.shape[1] // pool_width)]
    y = _mm(parts, w_out_even, w_out_even.shape[2], lead=(e,))
    return y, ckv[lay.t_lat:], kpe[lay.t_lat:]


def _odd_mixer(lay, h, o, layer, state_hgrn, w_in_odd, lb_fwd, lb_bwd, g_norm_gain, w_out_odd):
    z = _mm([(h, 0)], w_in_odd, w_in_odd.shape[2], lead=(o,))
    (r_lat,) = _gla(z, lb_fwd, lb_bwd, g_norm_gain[o], layer, n_seq=lay.nb_lat, seq_len=lay.l_lat, row0=0,
                    state_in=state_hgrn, odd=o)
    r_ctx, state = _gla(z, lb_fwd, lb_bwd, g_norm_gain[o], layer, n_seq=lay.nb_ctx, seq_len=lay.l_ctx,
                        row0=lay.t_lat, want_state=True)
    r = jnp.concatenate([r_lat, r_ctx], axis=0)
    return _mm([(r, 0)], w_out_odd, w_out_odd.shape[2], lead=(o,)), state


def kernel(x_prompt, x_sample, cache_ckv, cache_kpe, state_hgrn, c, c_ctx, w_ada, b_ada, ffn_w1, ffn_w3, ffn_w2, w_in_even, q_a_gain, kv_a_gain, w_qb, w_kvb, w_pool, pool_scale, w_out_even, w_in_odd, lb_fwd, lb_bwd, g_norm_gain, w_out_odd, final_gain):
    nb_ctx, l_ctx, d = x_prompt.shape
    nb_lat, l_lat, _ = x_sample.shape
    depth = w_ada.shape[0]
    lay = _Layout(nb_lat, l_lat, nb_ctx, l_ctx)
    assert nb_lat + 1 <= MOD_ROWS and l_lat % ROW_TILE == 0 and l_ctx == ROW_TILE
    assert (nb_lat * (cache_ckv.shape[2] + l_lat)) % l_ctx == 0

    x = jnp.concatenate([x_sample.reshape(lay.t_lat, d), x_prompt.reshape(lay.t_ctx, d)], axis=0)
    cmat = jnp.concatenate([c, c_ctx[None, :], jnp.zeros((MOD_ROWS - nb_lat - 1, d), F32)], axis=0)
    mods = _ada_mods(cmat, w_ada, b_ada).reshape(depth, MOD_ROWS, N_MOD, d)

    new_ckv, new_kpe, new_states = [], [], []
    x, h = _transition(lay, mods, x, mod=(0, 0))
    for layer in range(depth):
        y = _ffn(h, ffn_w1, ffn_w3, ffn_w2, layer, 0)
        x, h = _transition(lay, mods, x, y, gate=(layer, 2, 0.5), mod=(layer, 3))
        if layer % 2 == 0:
            y, ckv, kpe = _even_mixer(lay, h, layer // 2, cache_ckv, cache_kpe, w_in_even, q_a_gain, kv_a_gain,
                                      w_qb, w_kvb, w_pool, pool_scale, w_out_even)
            new_ckv.append(ckv.reshape(nb_ctx, l_ctx, -1))
            new_kpe.append(kpe.reshape(nb_ctx, l_ctx, -1))
        else:
            y, state = _odd_mixer(lay, h, layer // 2, layer, state_hgrn, w_in_odd, lb_fwd, lb_bwd,
                                  g_norm_gain, w_out_odd)
            new_states.append(state)
        x, h = _transition(lay, mods, x, y, gate=(layer, 5, 1.0), mod=(layer, 6))
        y = _ffn(h, ffn_w1, ffn_w3, ffn_w2, layer, 1)
        if layer + 1 < depth:
            x, h = _transition(lay, mods, x, y, gate=(layer, 8, 0.5), mod=(layer + 1, 0))
    gate = (depth - 1, 8, 0.5)
    lat_tiles = lay.t_lat // ROW_TILE
    y_sample = _transition(lay, mods, x, y, gate=gate, final_gain=final_gain, rows=(0, lat_tiles))
    y_prompt = _transition(lay, mods, x, y, gate=gate, final_gain=final_gain,
                           rows=(lat_tiles, lay.t_ctx // ROW_TILE))
    return (y_prompt.reshape(nb_ctx, l_ctx, d), y_sample.reshape(nb_lat, l_lat, d),
            jnp.stack(new_ckv, axis=1), jnp.stack(new_kpe, axis=1), jnp.stack(new_states, axis=1))
```

```python
import functools

import jax
import jax.numpy as jnp
from jax import lax
from jax.experimental import pallas as pl
from jax.experimental.pallas import tpu as pltpu

F32 = jnp.float32
BF16 = jnp.bfloat16

EPS = 1e-6
N_MOD = 9
GRID_W = 64
ROPE_BASE = 10000.0
POOL_WINDOWS = (2, 4, 8, 16)
POOL_HALO = 8
QK_NOPE = 128
QK_ROPE = 64
V_HEAD = 128
REC_HEAD = 128
CHUNK = 64
GLA_UNROLL = 4
PACKED_SUBLANES = 16
ROW_TILE = 256
MOD_ROWS = 8
MIB = 1 << 20
LOG2_E = 1.4426950408889634


def _params(semantics, vmem_mib):
    return pltpu.CompilerParams(dimension_semantics=semantics, vmem_limit_bytes=vmem_mib * MIB)


class _Layout:
    def __init__(self, nb_lat, l_lat, nb_ctx, l_ctx):
        self.nb_lat, self.l_lat, self.nb_ctx, self.l_ctx = nb_lat, l_lat, nb_ctx, l_ctx
        self.t_lat = nb_lat * l_lat
        self.t_ctx = nb_ctx * l_ctx
        self.t = self.t_lat + self.t_ctx

    def group(self, i, rows):
        per = self.l_lat // rows
        return jnp.where(i < self.nb_lat * per, i // per, self.nb_lat)


def _fit(n, tile):
    if n <= tile:
        return n
    return max(d for d in range(PACKED_SUBLANES, tile + 1, PACKED_SUBLANES) if n % d == 0)


def _silu(x):
    return x * jax.nn.sigmoid(x)


def _rms(x):
    return x * lax.rsqrt(jnp.mean(x * x, axis=-1, keepdims=True) + EPS)


def _ada_kernel(c_ref, w_ref, b_ref, o_ref):
    s = _silu(c_ref[...]).astype(BF16)
    o_ref[...] = jnp.dot(s, w_ref[...].astype(BF16), preferred_element_type=F32) + b_ref[...]


def _ada_mods(cmat, w_ada, b_ada, tn=512):
    depth, d, n = w_ada.shape
    return pl.pallas_call(
        _ada_kernel,
        grid=(depth, n // tn),
        in_specs=[pl.BlockSpec((MOD_ROWS, d), lambda l, j: (0, 0)),
                  pl.BlockSpec((None, d, tn), lambda l, j: (l, 0, j)),
                  pl.BlockSpec((None, 1, tn), lambda l, j: (l, 0, j))],
        out_specs=pl.BlockSpec((None, MOD_ROWS, tn), lambda l, j: (l, 0, j)),
        out_shape=jax.ShapeDtypeStruct((depth, MOD_ROWS, n), F32),
        compiler_params=_params(("arbitrary", "arbitrary"), 40),
        name="ada_mods",
    )(cmat, w_ada, b_ada.reshape(depth, 1, n))


def _trans_kernel(*refs, has_y, coef, gate_idx, mod_idx, final):
    it = iter(refs)
    x = next(it)[...]
    if has_y:
        y_ref, mg_ref = next(it), next(it)
        g = mg_ref[gate_idx:gate_idx + 1, :]
        if coef != 1.0:
            g = coef * g
        x = x + g * y_ref[...]
    xn = _rms(x)
    if final:
        gain_ref, o_ref = next(it), next(it)
        o_ref[...] = xn * gain_ref[...]
        return
    mm_ref = next(it)
    shift = mm_ref[mod_idx:mod_idx + 1, :]
    scale = mm_ref[mod_idx + 1:mod_idx + 2, :]
    if has_y:
        next(it)[...] = x
    next(it)[...] = (xn * (1.0 + scale) + shift).astype(BF16)


def _transition(lay, mods, x, y=None, gate=None, mod=None, final_gain=None, rows=None):
    t, d = x.shape
    tr = ROW_TILE
    off, n_tiles = rows if rows is not None else (0, t // tr)
    row_spec = pl.BlockSpec((tr, d), lambda i: (i + off, 0))

    def mod_spec(layer):
        return pl.BlockSpec((None, None, N_MOD, d), lambda i: (layer, lay.group(i + off, tr), 0, 0))

    args, in_specs = [x], [row_spec]
    has_y = y is not None
    coef, gate_idx, mod_idx = 1.0, 0, 0
    if has_y:
        gate_layer, gate_idx, coef = gate
        args += [y, mods]
        in_specs += [row_spec, mod_spec(gate_layer)]
    if final_gain is not None:
        args.append(final_gain.reshape(1, d))
        in_specs.append(pl.BlockSpec((1, d), lambda i: (0, 0)))
        out_shape = jax.ShapeDtypeStruct((n_tiles * tr, d), F32)
        out_specs = pl.BlockSpec((tr, d), lambda i: (i, 0))
    else:
        mod_layer, mod_idx = mod
        args.append(mods)
        in_specs.append(mod_spec(mod_layer))
        out_shape = [jax.ShapeDtypeStruct((t, d), BF16)]
        out_specs = [pl.BlockSpec((tr, d), lambda i: (i, 0))]
        if has_y:
            out_shape.insert(0, jax.ShapeDtypeStruct((t, d), F32))
            out_specs.insert(0, pl.BlockSpec((tr, d), lambda i: (i, 0)))
    out = pl.pallas_call(
        functools.partial(_trans_kernel, has_y=has_y, coef=coef, gate_idx=gate_idx, mod_idx=mod_idx,
                          final=final_gain is not None),
        grid=(n_tiles,),
        in_specs=in_specs, out_specs=out_specs, out_shape=out_shape,
        compiler_params=_params(("arbitrary",), 48),
        name="transition",
    )(*args)
    if final_gain is not None:
        return out
    return (out[0], out[1]) if has_y else (x, out[0])


def _ffn_kernel(h_ref, w1_ref, w3_ref, w2_ref, o_ref):
    @pl.when(pl.program_id(1) == 0)
    def _():
        o_ref[...] = jnp.zeros_like(o_ref)

    h = h_ref[...]
    a1 = jnp.dot(h, w1_ref[...].astype(BF16), preferred_element_type=F32)
    a3 = jnp.dot(h, w3_ref[...].astype(BF16), preferred_element_type=F32)
    g = (_silu(a1) * a3).astype(BF16)
    o_ref[...] += jnp.dot(g, w2_ref[...].astype(BF16), preferred_element_type=F32)


def _ffn(h, w1, w3, w2, layer, half, tm=1024, tf=256):
    t, d = h.shape
    dff = w1.shape[-1]
    tm = _fit(t, tm)
    return pl.pallas_call(
        _ffn_kernel,
        grid=(t // tm, dff // tf),
        in_specs=[pl.BlockSpec((tm, d), lambda i, j: (i, 0)),
                  pl.BlockSpec((None, None, d, tf), lambda i, j: (layer, half, 0, j)),
                  pl.BlockSpec((None, None, d, tf), lambda i, j: (layer, half, 0, j)),
                  pl.BlockSpec((None, None, tf, d), lambda i, j: (layer, half, j, 0))],
        out_specs=pl.BlockSpec((tm, d), lambda i, j: (i, 0), pipeline_mode=pl.Buffered(1)),
        out_shape=jax.ShapeDtypeStruct((t, d), F32),
        compiler_params=_params(("arbitrary", "arbitrary"), 60),
        name="ffn",
    )(h, w1, w3, w2)


def _mm_kernel(*refs, n_parts, has_gain, resident):
    x_refs, w_refs = refs[:n_parts], refs[n_parts:2 * n_parts]
    if resident:
        wb_ref = refs[-1]
        o_ref = refs[-2]

        @pl.when(pl.program_id(0) == 0)
        def _():
            wb_ref[...] = w_refs[0][...].astype(BF16)

        acc = jnp.dot(x_refs[0][...].astype(BF16), wb_ref[...], preferred_element_type=F32)
    else:
        o_ref = refs[-1]
        acc = None
        for x_ref, w_ref in zip(x_refs, w_refs):
            part = jnp.dot(x_ref[...].astype(BF16), w_ref[...].astype(BF16), preferred_element_type=F32)
            acc = part if acc is None else acc + part
    if has_gain:
        acc = _rms(acc) * refs[2 * n_parts][...]
    o_ref[...] = acc.astype(o_ref.dtype)


def _mm(parts, w, n_out, *, lead=(), tm=1024, tn=512, out_dtype=F32, gain=None, vmem_mib=56):
    t = parts[0][0].shape[0]
    kp = w.shape[-2] // len(parts)
    tm, tn = _fit(t, tm), _fit(n_out, tn)
    if gain is not None:
        assert tn == n_out
    resident = tn == n_out and len(parts) == 1
    nl = (None,) * len(lead)
    w_mode = dict(pipeline_mode=pl.Buffered(1)) if resident else {}
    in_specs = [pl.BlockSpec((tm, kp), functools.partial(lambda i, j, cb: (i, cb), cb=cb)) for _, cb in parts]
    in_specs += [pl.BlockSpec(nl + (kp, tn), functools.partial(lambda i, j, p: lead + (p, j), p=p), **w_mode)
                 for p in range(len(parts))]
    args = [a for a, _ in parts] + [w] * len(parts)
    if gain is not None:
        args.append(gain.reshape(1, n_out))
        in_specs.append(pl.BlockSpec((1, n_out), lambda i, j: (0, 0)))
    return pl.pallas_call(
        functools.partial(_mm_kernel, n_parts=len(parts), has_gain=gain is not None, resident=resident),
        grid=(t // tm, n_out // tn),
        in_specs=in_specs,
        out_specs=pl.BlockSpec((tm, tn), lambda i, j: (i, j)),
        out_shape=jax.ShapeDtypeStruct((t, n_out), out_dtype),
        scratch_shapes=[pltpu.VMEM((kp, tn), BF16)] if resident else [],
        compiler_params=_params(("arbitrary", "arbitrary"), vmem_mib),
        name="proj",
    )(*args)


def _pool_kernel(a_ref, prev_ref, next_ref, w_ref, s_ref, o_ref, pad_ref, *, lay):
    tr = ROW_TILE
    grp = a_ref.shape[1] // len(POOL_WINDOWS)
    i = pl.program_id(0)
    per = lay.l_lat // tr
    is_lat = i < lay.nb_lat * per
    k = i % per
    has_prev = jnp.logical_and(is_lat, k != 0)
    has_next = jnp.logical_and(is_lat, k != per - 1)
    pad_ref[0:POOL_HALO, :] = jnp.where(has_prev, prev_ref[...], 0.0)
    pad_ref[POOL_HALO:POOL_HALO + tr, :] = a_ref[...]
    pad_ref[POOL_HALO + tr:, :] = jnp.where(has_next, next_ref[...], 0.0)
    t = jnp.where(is_lat, k * tr, 0) + lax.broadcasted_iota(jnp.int32, (tr, 1), 0)
    seq_len = jnp.where(is_lat, lay.l_lat, lay.l_ctx)
    for g, win in enumerate(POOL_WINDOWS):
        cols = slice(g * grp, (g + 1) * grp)
        acc = None
        for dlt in range(-(win // 2), win // 2):
            v = pad_ref[POOL_HALO + dlt:POOL_HALO + dlt + tr, cols]
            acc = v if acc is None else acc + v
        cnt = (jnp.clip(t + win // 2, 0, seq_len) - jnp.clip(t - win // 2, 0, seq_len)).astype(F32)
        p = acc / cnt - a_ref[:, cols]
        y = jnp.dot(p.astype(BF16), w_ref[g].astype(BF16), preferred_element_type=F32) * s_ref[:, cols]
        o_ref[:, cols] = y.astype(o_ref.dtype)


def _pool_mixer(lay, a, w_pool, pool_scale, e):
    t, width = a.shape
    tr, hb = ROW_TILE, ROW_TILE // POOL_HALO
    n_halo = t // POOL_HALO
    grp = width // len(POOL_WINDOWS)
    return pl.pallas_call(
        functools.partial(_pool_kernel, lay=lay),
        grid=(t // tr,),
        in_specs=[pl.BlockSpec((tr, width), lambda i: (i, 0)),
                  pl.BlockSpec((POOL_HALO, width), lambda i: (jnp.maximum(i * hb - 1, 0), 0)),
                  pl.BlockSpec((POOL_HALO, width), lambda i: (jnp.minimum((i + 1) * hb, n_halo - 1), 0)),
                  pl.BlockSpec((None, len(POOL_WINDOWS), grp, grp), lambda i: (e, 0, 0, 0)),
                  pl.BlockSpec((None, 1, width), lambda i: (e, 0, 0))],
        out_specs=pl.BlockSpec((tr, width), lambda i: (i, 0)),
        out_shape=jax.ShapeDtypeStruct((t, width), BF16),
        scratch_shapes=[pltpu.VMEM((tr + 2 * POOL_HALO, width), F32)],
        compiler_params=_params(("arbitrary",), 32),
        name="pool_mixer",
    )(a, a, a, w_pool, pool_scale.reshape(pool_scale.shape[0], 1, width))


def _rope_swap(x):
    lane = lax.broadcasted_iota(jnp.int32, (1, x.shape[-1]), 1)
    half = QK_ROPE // 4
    return jnp.where(lane % (2 * half) < half, pltpu.roll(x, x.shape[-1] - half, 1), pltpu.roll(x, half, 1))


def _attn_kernel(*refs, rope, scale, pairs):
    qn_ref, qp_ref, kv_ref, kpe_ref = refs[:4]
    o_ref = refs[-1]
    kpe = kpe_ref[...]
    if rope:
        cq_ref, sq_ref, ck_ref, sk_ref = refs[4:8]
        kpe = kpe * ck_ref[...] + _rope_swap(kpe) * sk_ref[...]
    kpe = kpe.astype(BF16)
    lane = lax.broadcasted_iota(jnp.int32, (1, 2 * QK_ROPE), 1)
    for pair in range(pairs):
        qp = qp_ref[:, pair * 2 * QK_ROPE:(pair + 1) * 2 * QK_ROPE]
        if rope:
            qp = qp * cq_ref[...] + _rope_swap(qp) * sq_ref[...]
        for hh in range(2):
            head = 2 * pair + hh
            own = jnp.logical_and(lane >= hh * QK_ROPE, lane < (hh + 1) * QK_ROPE)
            q = jnp.concatenate([qn_ref[:, head * QK_NOPE:(head + 1) * QK_NOPE], jnp.where(own, qp, 0.0)],
                                axis=-1)
            base = head * (QK_NOPE + V_HEAD)
            k = jnp.concatenate([kv_ref[:, base:base + QK_NOPE], kpe], axis=-1)
            v = kv_ref[:, base + QK_NOPE:base + QK_NOPE + V_HEAD]
            s = lax.dot_general(q.astype(BF16), k, (((1,), (1,)), ((), ())), preferred_element_type=F32)
            p = jnp.exp2((s - jnp.max(s, axis=-1, keepdims=True)) * (scale * LOG2_E))
            p = p * (1.0 / jnp.sum(p, axis=-1, keepdims=True))
            o = jnp.dot(p.astype(BF16), v, preferred_element_type=F32)
            o_ref[:, head * V_HEAD:(head + 1) * V_HEAD] = o.astype(o_ref.dtype)


def _attention(q, kv, kpe2, n_heads, *, n_seq, lq, lk, q_row0, kv_row0, tq, out_rows, tables=None, prev=None,
               pairs=4):
    while (n_heads // 2) % pairs:
        pairs -= 1
    n_groups = n_heads // (2 * pairs)
    qt = lq // tq
    rope_col0 = n_heads * QK_NOPE // (2 * QK_ROPE * pairs)
    scale = float(QK_NOPE + QK_ROPE) ** -0.5

    def q_row(b, qi):
        return q_row0 // tq + b * qt + qi

    in_specs = [pl.BlockSpec((tq, 2 * QK_NOPE * pairs), lambda b, g, qi: (q_row(b, qi), g)),
                pl.BlockSpec((tq, 2 * QK_ROPE * pairs), lambda b, g, qi: (q_row(b, qi), rope_col0 + g)),
                pl.BlockSpec((lk, 2 * (QK_NOPE + V_HEAD) * pairs), lambda b, g, qi: (kv_row0 // lk + b, g)),
                pl.BlockSpec((lk, 2 * QK_ROPE), lambda b, g, qi: (kv_row0 // lk + b, 0))]
    args = [q, q, kv, kpe2]
    if tables is not None:
        cq, sq, ck, sk = tables
        in_specs += [pl.BlockSpec((tq, 2 * QK_ROPE), lambda b, g, qi: (qi, 0)),
                     pl.BlockSpec((tq, 2 * QK_ROPE), lambda b, g, qi: (qi, 0)),
                     pl.BlockSpec((lk, 2 * QK_ROPE), lambda b, g, qi: (0, 0)),
                     pl.BlockSpec((lk, 2 * QK_ROPE), lambda b, g, qi: (0, 0))]
        args += [cq, sq, ck, sk]
    aliases = {}
    if prev is not None:
        aliases = {len(args): 0}
        in_specs.append(pl.BlockSpec(memory_space=pl.ANY))
        args.append(prev)
    return pl.pallas_call(
        functools.partial(_attn_kernel, rope=tables is not None, scale=scale, pairs=pairs),
        grid=(n_seq, n_groups, qt),
        in_specs=in_specs,
        out_specs=pl.BlockSpec((tq, 2 * V_HEAD * pairs), lambda b, g, qi: (q_row(b, qi), g)),
        out_shape=jax.ShapeDtypeStruct((out_rows, n_heads * V_HEAD), BF16),
        input_output_aliases=aliases,
        compiler_params=_params(("arbitrary", "arbitrary", "arbitrary"), 48),
        name="attention",
    )(*args)


def _rope_tables(l_lat, past):
    t = jnp.arange(l_lat)
    row = (t // GRID_W).astype(F32)
    col = (t % GRID_W).astype(F32)
    axis = QK_ROPE // 2
    inv = 1.0 / (ROPE_BASE ** (jnp.arange(0, axis, 2, dtype=F32) / axis))
    ar, ac = row[:, None] * inv, col[:, None] * inv
    cos = jnp.concatenate([jnp.cos(ar), jnp.cos(ar), jnp.cos(ac), jnp.cos(ac)], axis=-1)
    sin = jnp.concatenate([-jnp.sin(ar), jnp.sin(ar), -jnp.sin(ac), jnp.sin(ac)], axis=-1)
    cq, sq = jnp.tile(cos, (1, 2)), jnp.tile(sin, (1, 2))
    ck = jnp.concatenate([jnp.ones((past, 2 * QK_ROPE), F32), cq], axis=0)
    sk = jnp.concatenate([jnp.zeros((past, 2 * QK_ROPE), F32), sq], axis=0)
    return cq, sq, ck, sk


def _lower_bound(lb, layer):
    rows = [lb[i:i + 1, :] for i in range(lb.shape[0])]
    m = functools.reduce(jnp.maximum, rows)
    e = [jnp.exp(r - m) for r in rows]
    total = functools.reduce(lambda a, b: a + b, e)
    p = [x / total for x in e]
    cum = functools.reduce(lambda a, b: a + b, p[:layer + 1])
    return cum - p[0]


def _cumsum_rows(x, tri):
    hi = x.astype(BF16)
    r1 = x - hi.astype(F32)
    mid = r1.astype(BF16)
    lo = (r1 - mid.astype(F32)).astype(BF16)
    return (jnp.dot(tri, hi, preferred_element_type=F32) + jnp.dot(tri, mid, preferred_element_type=F32)
            + jnp.dot(tri, lo, preferred_element_type=F32))


def _gla_kernel(*refs, layer, n_chunks, hg, has_state_in, has_state_out, n_prev):
    it = iter(refs)
    zq_ref, zf_ref, zb_ref, zi_ref, zg_ref, lbf_ref, lbb_ref, gain_ref = [next(it) for _ in range(8)]
    s_in_ref = next(it) if has_state_in else None
    for _ in range(n_prev):
        next(it)
    r_ref = next(it)
    s_out_ref = next(it) if has_state_out else None
    of_ref, ob_ref, st_ref, q_ref = next(it), next(it), next(it), next(it)
    q_ref[...] = _silu(zq_ref[...])

    row = lax.broadcasted_iota(jnp.int32, (CHUNK, CHUNK), 0)
    col = lax.broadcasted_iota(jnp.int32, (CHUNK, CHUNK), 1)
    masks = (col <= row, col >= row)
    tris = tuple(jnp.where(m, 1.0, 0.0).astype(BF16) for m in masks)
    gates = []
    for lb_ref in (lbf_ref, lbb_ref):
        lb = _lower_bound(lb_ref[...], layer)
        gates.append((jnp.log(lb), jnp.log1p(-lb), 1.0 - lb))

    for dr in range(2):
        for h in range(hg):
            st_ref[dr, h] = s_in_ref[dr, h].T if has_state_in else jnp.zeros((REC_HEAD, REC_HEAD), F32)

    def step(c, rev):
        r0 = c * CHUNK if isinstance(c, int) else pl.multiple_of(c * CHUNK, CHUNK)
        rows = pl.ds(r0, CHUNK)
        log_lb, log_1m_lb, one_m_lb = gates[rev]
        q = q_ref[rows, :]
        z = (zb_ref if rev else zf_ref)[rows, :]
        t = jnp.exp(-jnp.abs(z))
        one_p_t = 1.0 + t
        log_sig = jnp.minimum(z, 0.0) - jnp.log(one_p_t)
        u = log_1m_lb + log_sig
        log_f = jnp.maximum(log_lb, u) + jnp.log(1.0 + jnp.exp(-jnp.abs(log_lb - u)))
        inv = 1.0 / one_p_t
        k = one_m_lb * jnp.where(z > 0.0, t * inv, inv)
        v = zi_ref[rows, :].astype(BF16)
        b = _cumsum_rows(log_f, tris[rev])
        b_mid = b[CHUNK // 2:CHUNK // 2 + 1] if rev else b[CHUNK // 2 - 1:CHUNK // 2]
        b_last = b[0:1] if rev else b[CHUNK - 1:CHUNK]
        q_intra = (q * jnp.exp(b - b_mid)).astype(BF16)
        k_intra = (k * jnp.exp(b_mid - b)).astype(BF16)
        q_in = (q * jnp.exp(b)).astype(BF16)
        k_out = (k * jnp.exp(b_last - b)).astype(BF16)
        decay = jnp.exp(b_last)
        o_dst = ob_ref if rev else of_ref
        for h in range(hg):
            sl = slice(h * REC_HEAD, (h + 1) * REC_HEAD)
            sc = lax.dot_general(q_intra[:, sl], k_intra[:, sl], (((1,), (1,)), ((), ())),
                                 preferred_element_type=F32)
            sc = jnp.where(masks[rev], sc, 0.0).astype(BF16)
            st = st_ref[rev, h]
            o = (jnp.dot(sc, v[:, sl], preferred_element_type=F32)
                 + lax.dot_general(q_in[:, sl], st.astype(BF16), (((1,), (1,)), ((), ())),
                                   preferred_element_type=F32))
            st_ref[rev, h] = st * decay[:, sl] + lax.dot_general(
                v[:, sl], k_out[:, sl], (((0,), (0,)), ((), ())), preferred_element_type=F32)
            o_dst[rows, sl] = o

    def group(c0):
        for u in range(GLA_UNROLL):
            step(c0 + u, 0)
            step(n_chunks - 1 - (c0 + u), 1)

    if n_chunks == GLA_UNROLL:
        group(0)
    else:
        def body(i, carry):
            group(i * GLA_UNROLL)
            return carry
        lax.fori_loop(0, n_chunks // GLA_UNROLL, body, 0)

    gate = _silu(zg_ref[...])
    o = of_ref[...] + ob_ref[...]
    for h in range(hg):
        sl = slice(h * REC_HEAD, (h + 1) * REC_HEAD)
        r_ref[:, sl] = (_rms(o[:, sl]) * gain_ref[...] * gate[:, sl]).astype(r_ref.dtype)
    if has_state_out:
        for dr in range(2):
            for h in range(hg):
                s_out_ref[dr, h] = st_ref[dr, h].T


def _gla(z, lb_fwd, lb_bwd, g_gain, layer, *, n_seq, seq_len, row0, odd, n_odd, state_in=None, want_state=False,
         prev_r=None, prev_state=None, hg=4):
    width = z.shape[1] // 5
    n_heads = width // REC_HEAD
    w = hg * REC_HEAD
    ncb = width // w
    n_chunks = seq_len // CHUNK
    assert n_chunks % GLA_UNROLL == 0

    def z_spec(part):
        return pl.BlockSpec((seq_len, w), lambda s, g: (row0 // seq_len + s, part * ncb + g))

    in_specs = [z_spec(p) for p in range(5)]
    in_specs += [pl.BlockSpec((lb_fwd.shape[0], w), lambda s, g: (0, g)),
                 pl.BlockSpec((lb_bwd.shape[0], w), lambda s, g: (0, g)),
                 pl.BlockSpec((1, REC_HEAD), lambda s, g: (0, 0))]
    args = [z] * 5 + [lb_fwd, lb_bwd, g_gain.reshape(1, REC_HEAD)]
    if state_in is not None:
        in_specs.append(pl.BlockSpec((None, None, 2, hg, REC_HEAD, REC_HEAD), lambda s, g: (s, odd, 0, g, 0, 0)))
        args.append(state_in)
    out_shape = [jax.ShapeDtypeStruct((z.shape[0], width), BF16)]
    out_specs = [pl.BlockSpec((seq_len, w), lambda s, g: (row0 // seq_len + s, g))]
    if want_state:
        out_shape.append(jax.ShapeDtypeStruct((n_seq, n_odd, 2, n_heads, REC_HEAD, REC_HEAD), F32))
        out_specs.append(pl.BlockSpec((None, None, 2, hg, REC_HEAD, REC_HEAD), lambda s, g: (s, odd, 0, g, 0, 0)))
    aliases = {}
    for out_idx, prev in enumerate((prev_r, prev_state)):
        if prev is not None:
            aliases[len(args)] = out_idx
            in_specs.append(pl.BlockSpec(memory_space=pl.ANY))
            args.append(prev)
    return pl.pallas_call(
        functools.partial(_gla_kernel, layer=layer, n_chunks=n_chunks, hg=hg,
                          has_state_in=state_in is not None, has_state_out=want_state, n_prev=len(aliases)),
        grid=(n_seq, ncb),
        in_specs=in_specs, out_specs=out_specs, out_shape=out_shape,
        input_output_aliases=aliases,
        scratch_shapes=[pltpu.VMEM((seq_len, w), F32), pltpu.VMEM((seq_len, w), F32),
                        pltpu.VMEM((2, hg, REC_HEAD, REC_HEAD), F32), pltpu.VMEM((seq_len, w), F32)],
        compiler_params=_params(("arbitrary", "arbitrary"), 48),
        name="hgrn2",
    )(*args)


def _even_mixer(lay, h, e, cache_ckv, cache_kpe, w_in_even, q_a_gain, kv_a_gain, w_qb, w_kvb, w_pool,
                pool_scale, w_out_even):
    pool_width = pool_scale.shape[1]
    q_lora, kv_lora = q_a_gain.shape[1], kv_a_gain.shape[1]
    n_heads = w_qb.shape[2] // (QK_NOPE + QK_ROPE)
    past = cache_ckv.shape[2]
    o1, o2, o3 = pool_width, pool_width + q_lora, pool_width + q_lora + kv_lora

    a = _mm([(h, 0)], w_in_even, pool_width, lead=(e,), tm=512, tn=pool_width)
    cq = _mm([(h, 0)], w_in_even[e, :, o1:o2], q_lora, tn=q_lora, out_dtype=BF16, gain=q_a_gain[e])
    ckv = _mm([(h, 0)], w_in_even[e, :, o2:o3], kv_lora, tn=kv_lora, gain=kv_a_gain[e])
    kpe = _mm([(h, 0)], w_in_even[e, :, o3:], QK_ROPE)

    wq = w_qb[e].reshape(q_lora, n_heads, QK_NOPE + QK_ROPE)
    wq = jnp.concatenate([wq[:, :, :QK_NOPE].reshape(q_lora, -1), wq[:, :, QK_NOPE:].reshape(q_lora, -1)], axis=1)
    q = _mm([(cq, 0)], wq, wq.shape[1], tn=1536)

    def with_cache(cache, new):
        lat = jnp.concatenate([cache, new[:lay.t_lat].reshape(lay.nb_lat, lay.l_lat, -1)], axis=1)
        return jnp.concatenate([lat.reshape(lay.nb_lat * (past + lay.l_lat), -1), new[lay.t_lat:]], axis=0)

    ckv_all = with_cache(cache_ckv[:, e], ckv)
    kpe_all = jnp.tile(with_cache(cache_kpe[:, e], kpe), (1, 2))
    kv = _mm([(ckv_all, 0)], w_kvb, w_kvb.shape[2], lead=(e,), tm=1664, tn=1024, out_dtype=BF16)

    lk_lat = past + lay.l_lat
    att = _attention(q, kv, kpe_all, n_heads, n_seq=lay.nb_lat, lq=lay.l_lat, lk=lk_lat, q_row0=0, kv_row0=0,
                     tq=ROW_TILE, out_rows=lay.t, tables=_rope_tables(lay.l_lat, past))
    att = _attention(q, kv, kpe_all, n_heads, n_seq=lay.nb_ctx, lq=lay.l_ctx, lk=lay.l_ctx, q_row0=lay.t_lat,
                     kv_row0=lay.nb_lat * lk_lat, tq=lay.l_ctx, out_rows=lay.t, prev=att)
    pooled = _pool_mixer(lay, a, w_pool, pool_scale, e)

    parts = [(pooled, 0)] + [(att, cb) for cb in range(att.shape[1] // pool_width)]
    y = _mm(parts, w_out_even, w_out_even.shape[2], lead=(e,))
    return y, ckv[lay.t_lat:], kpe[lay.t_lat:]


def _odd_mixer(lay, h, o, layer, state_hgrn, prev_states, w_in_odd, lb_fwd, lb_bwd, g_norm_gain, w_out_odd):
    n_odd = w_in_odd.shape[0]
    z = _mm([(h, 0)], w_in_odd, w_in_odd.shape[2], lead=(o,))
    (r,) = _gla(z, lb_fwd, lb_bwd, g_norm_gain[o], layer, n_seq=lay.nb_lat, seq_len=lay.l_lat, row0=0,
                odd=o, n_odd=n_odd, state_in=state_hgrn)
    r, states = _gla(z, lb_fwd, lb_bwd, g_norm_gain[o], layer, n_seq=lay.nb_ctx, seq_len=lay.l_ctx,
                     row0=lay.t_lat, odd=o, n_odd=n_odd, want_state=True, prev_r=r, prev_state=prev_states)
    return _mm([(r, 0)], w_out_odd, w_out_odd.shape[2], lead=(o,)), states


def kernel(x_prompt, x_sample, cache_ckv, cache_kpe, state_hgrn, c, c_ctx, w_ada, b_ada, ffn_w1, ffn_w3, ffn_w2, w_in_even, q_a_gain, kv_a_gain, w_qb, w_kvb, w_pool, pool_scale, w_out_even, w_in_odd, lb_fwd, lb_bwd, g_norm_gain, w_out_odd, final_gain):
    nb_ctx, l_ctx, d = x_prompt.shape
    nb_lat, l_lat, _ = x_sample.shape
    depth = w_ada.shape[0]
    lay = _Layout(nb_lat, l_lat, nb_ctx, l_ctx)
    assert nb_lat + 1 <= MOD_ROWS and l_lat % ROW_TILE == 0 and l_ctx == ROW_TILE
    assert (nb_lat * (cache_ckv.shape[2] + l_lat)) % l_ctx == 0

    x = jnp.concatenate([x_sample.reshape(lay.t_lat, d), x_prompt.reshape(lay.t_ctx, d)], axis=0)
    cmat = jnp.concatenate([c, c_ctx[None, :], jnp.zeros((MOD_ROWS - nb_lat - 1, d), F32)], axis=0)
    mods = _ada_mods(cmat, w_ada, b_ada).reshape(depth, MOD_ROWS, N_MOD, d)

    new_ckv, new_kpe, new_states = [], [], None
    x, h = _transition(lay, mods, x, mod=(0, 0))
    for layer in range(depth):
        y = _ffn(h, ffn_w1, ffn_w3, ffn_w2, layer, 0)
        x, h = _transition(lay, mods, x, y, gate=(layer, 2, 0.5), mod=(layer, 3))
        if layer % 2 == 0:
            y, ckv, kpe = _even_mixer(lay, h, layer // 2, cache_ckv, cache_kpe, w_in_even, q_a_gain, kv_a_gain,
                                      w_qb, w_kvb, w_pool, pool_scale, w_out_even)
            new_ckv.append(ckv.reshape(nb_ctx, l_ctx, -1))
            new_kpe.append(kpe.reshape(nb_ctx, l_ctx, -1))
        else:
            y, new_states = _odd_mixer(lay, h, layer // 2, layer, state_hgrn, new_states, w_in_odd, lb_fwd,
                                       lb_bwd, g_norm_gain, w_out_odd)
        x, h = _transition(lay, mods, x, y, gate=(layer, 5, 1.0), mod=(layer, 6))
        y = _ffn(h, ffn_w1, ffn_w3, ffn_w2, layer, 1)
        if layer + 1 < depth:
            x, h = _transition(lay, mods, x, y, gate=(layer, 8, 0.5), mod=(layer + 1, 0))
    gate = (depth - 1, 8, 0.5)
    lat_tiles = lay.t_lat // ROW_TILE
    y_sample = _transition(lay, mods, x, y, gate=gate, final_gain=final_gain, rows=(0, lat_tiles))
    y_prompt = _transition(lay, mods, x, y, gate=gate, final_gain=final_gain,
                           rows=(lat_tiles, lay.t_ctx // ROW_TILE))
    return (y_prompt.reshape(nb_ctx, l_ctx, d), y_sample.reshape(nb_lat, l_lat, d),
            jnp.stack(new_ckv, axis=1), jnp.stack(new_kpe, axis=1), new_states)
```

```python
import functools

import jax
import jax.numpy as jnp
from jax import lax
from jax.experimental import pallas as pl
from jax.experimental.pallas import tpu as pltpu

F32 = jnp.float32
BF16 = jnp.bfloat16

EPS = 1e-6
N_MOD = 9
GRID_W = 64
ROPE_BASE = 10000.0
POOL_WINDOWS = (2, 4, 8, 16)
POOL_HALO = 8
QK_NOPE = 128
QK_ROPE = 64
V_HEAD = 128
REC_HEAD = 128
CHUNK = 64
GLA_UNROLL = 4
PACKED_SUBLANES = 16
MXU_COLS = 256
ROW_TILE = 256
MOD_ROWS = 8
MIB = 1 << 20
LOG2_E = 1.4426950408889634


def _params(semantics, vmem_mib):
    return pltpu.CompilerParams(dimension_semantics=semantics, vmem_limit_bytes=vmem_mib * MIB)


class _Layout:
    def __init__(self, nb_lat, l_lat, nb_ctx, l_ctx):
        self.nb_lat, self.l_lat, self.nb_ctx, self.l_ctx = nb_lat, l_lat, nb_ctx, l_ctx
        self.t_lat = nb_lat * l_lat
        self.t_ctx = nb_ctx * l_ctx
        self.t = self.t_lat + self.t_ctx

    def group(self, i, rows):
        per = self.l_lat // rows
        return jnp.where(i < self.nb_lat * per, i // per, self.nb_lat)


def _fit(n, tile):
    if n <= tile:
        return n
    return max(d for d in range(PACKED_SUBLANES, tile + 1, PACKED_SUBLANES) if n % d == 0)


def _silu(x):
    return x * jax.nn.sigmoid(x)


def _rms(x):
    return x * lax.rsqrt(jnp.mean(x * x, axis=-1, keepdims=True) + EPS)


def _ada_kernel(c_ref, w_ref, b_ref, o_ref):
    s = _silu(c_ref[...]).astype(BF16)
    o_ref[...] = jnp.dot(s, w_ref[...].astype(BF16), preferred_element_type=F32) + b_ref[...]


def _ada_mods(cmat, w_ada, b_ada3, layer, tn=512):
    _, d, n = w_ada.shape
    return pl.pallas_call(
        _ada_kernel,
        grid=(n // tn,),
        in_specs=[pl.BlockSpec((MOD_ROWS, d), lambda j: (0, 0)),
                  pl.BlockSpec((None, d, tn), lambda j: (layer, 0, j)),
                  pl.BlockSpec((None, 1, tn), lambda j: (layer, 0, j))],
        out_specs=pl.BlockSpec((MOD_ROWS, tn), lambda j: (0, j)),
        out_shape=jax.ShapeDtypeStruct((MOD_ROWS, n), F32),
        compiler_params=_params(("arbitrary",), 40),
        name="ada_mods",
    )(cmat, w_ada, b_ada3)


def _trans_kernel(*refs, has_y, coef, gate_idx, mod_idx, final):
    it = iter(refs)
    x = next(it)[...]
    if has_y:
        y_ref, mg_ref = next(it), next(it)
        g = mg_ref[gate_idx:gate_idx + 1, :]
        if coef != 1.0:
            g = coef * g
        x = x + g * y_ref[...]
    xn = _rms(x)
    if final:
        gain_ref, o_ref = next(it), next(it)
        o_ref[...] = xn * gain_ref[...]
        return
    mm_ref = next(it)
    shift = mm_ref[mod_idx:mod_idx + 1, :]
    scale = mm_ref[mod_idx + 1:mod_idx + 2, :]
    if has_y:
        next(it)[...] = x
    next(it)[...] = (xn * (1.0 + scale) + shift).astype(BF16)


def _transition(lay, mods, x, y=None, gate=None, mod=None, final_gain=None, rows=None):
    t, d = x.shape
    tr = ROW_TILE
    off, n_tiles = rows if rows is not None else (0, t // tr)
    row_spec = pl.BlockSpec((tr, d), lambda i: (i + off, 0))
    mod_spec = pl.BlockSpec((None, N_MOD, d), lambda i: (lay.group(i + off, tr), 0, 0))

    args, in_specs = [x], [row_spec]
    has_y = y is not None
    coef, gate_idx, mod_idx = 1.0, 0, 0
    if has_y:
        gate_layer, gate_idx, coef = gate
        args += [y, mods[gate_layer]]
        in_specs += [row_spec, mod_spec]
    if final_gain is not None:
        args.append(final_gain.reshape(1, d))
        in_specs.append(pl.BlockSpec((1, d), lambda i: (0, 0)))
        out_shape = jax.ShapeDtypeStruct((n_tiles * tr, d), F32)
        out_specs = pl.BlockSpec((tr, d), lambda i: (i, 0))
    else:
        mod_layer, mod_idx = mod
        args.append(mods[mod_layer])
        in_specs.append(mod_spec)
        out_shape = [jax.ShapeDtypeStruct((t, d), BF16)]
        out_specs = [pl.BlockSpec((tr, d), lambda i: (i, 0))]
        if has_y:
            out_shape.insert(0, jax.ShapeDtypeStruct((t, d), F32))
            out_specs.insert(0, pl.BlockSpec((tr, d), lambda i: (i, 0)))
    out = pl.pallas_call(
        functools.partial(_trans_kernel, has_y=has_y, coef=coef, gate_idx=gate_idx, mod_idx=mod_idx,
                          final=final_gain is not None),
        grid=(n_tiles,),
        in_specs=in_specs, out_specs=out_specs, out_shape=out_shape,
        compiler_params=_params(("arbitrary",), 48),
        name="transition",
    )(*args)
    if final_gain is not None:
        return out
    return (out[0], out[1]) if has_y else (x, out[0])


def _ffn_kernel(*refs, n_side):
    h_ref, w1_ref, w3_ref, w2_ref = refs[:4]
    o_ref = refs[-2] if n_side else refs[-1]

    @pl.when(pl.program_id(1) == 0)
    def _():
        o_ref[...] = jnp.zeros_like(o_ref)

    h = h_ref[...]
    a1 = jnp.dot(h, w1_ref[...].astype(BF16), preferred_element_type=F32)
    a3 = jnp.dot(h, w3_ref[...].astype(BF16), preferred_element_type=F32)
    g = (_silu(a1) * a3).astype(BF16)
    o_ref[...] += jnp.dot(g, w2_ref[...].astype(BF16), preferred_element_type=F32)

    if n_side:
        c_ref, wa_ref, ba_ref = refs[4:7]
        m_ref = refs[-1]

        @pl.when(pl.program_id(0) * pl.num_programs(1) + pl.program_id(1) < n_side)
        def _():
            _ada_kernel(c_ref, wa_ref, ba_ref, m_ref)


def _side_tile(n_cols, steps):
    for tn in range(MXU_COLS, n_cols + 1, MXU_COLS):
        if n_cols % tn == 0 and n_cols // tn <= steps:
            return tn
    return None


def _ffn(h, w1, w3, w2, layer, half, side=None, tm=1024, tf=256):
    t, d = h.shape
    dff = w1.shape[-1]
    tm = _fit(t, tm)
    nj = dff // tf
    in_specs = [pl.BlockSpec((tm, d), lambda i, j: (i, 0), pipeline_mode=pl.Buffered(1)),
                pl.BlockSpec((None, None, d, tf), lambda i, j: (layer, half, 0, j)),
                pl.BlockSpec((None, None, d, tf), lambda i, j: (layer, half, 0, j)),
                pl.BlockSpec((None, None, tf, d), lambda i, j: (layer, half, j, 0))]
    args = [h, w1, w3, w2]
    out_specs = [pl.BlockSpec((tm, d), lambda i, j: (i, 0), pipeline_mode=pl.Buffered(1))]
    out_shape = [jax.ShapeDtypeStruct((t, d), F32)]
    n_side = 0
    if side is not None:
        cmat, w_ada, b_ada3, ada_layer, col0, n_cols, tn = side
        n_side = n_cols // tn

        def tile(i, j):
            return jnp.minimum(i * nj + j, n_side - 1)

        in_specs += [pl.BlockSpec((MOD_ROWS, d), lambda i, j: (0, 0)),
                     pl.BlockSpec((None, d, tn), lambda i, j: (ada_layer, 0, col0 // tn + tile(i, j))),
                     pl.BlockSpec((None, 1, tn), lambda i, j: (ada_layer, 0, col0 // tn + tile(i, j)))]
        args += [cmat, w_ada, b_ada3]
        out_specs.append(pl.BlockSpec((MOD_ROWS, tn), lambda i, j: (0, tile(i, j))))
        out_shape.append(jax.ShapeDtypeStruct((MOD_ROWS, n_cols), F32))
    out = pl.pallas_call(
        functools.partial(_ffn_kernel, n_side=n_side),
        grid=(t // tm, nj),
        in_specs=in_specs, out_specs=out_specs, out_shape=out_shape,
        compiler_params=_params(("arbitrary", "arbitrary"), 60),
        name="ffn",
    )(*args)
    return out if side is not None else out[0]


def _mm_kernel(*refs, n_parts, has_gain, resident):
    x_refs, w_refs = refs[:n_parts], refs[n_parts:2 * n_parts]
    if resident:
        wb_ref = refs[-1]
        o_ref = refs[-2]

        @pl.when(pl.program_id(0) == 0)
        def _():
            wb_ref[...] = w_refs[0][...].astype(BF16)

        acc = jnp.dot(x_refs[0][...].astype(BF16), wb_ref[...], preferred_element_type=F32)
    else:
        o_ref = refs[-1]
        acc = None
        for x_ref, w_ref in zip(x_refs, w_refs):
            part = jnp.dot(x_ref[...].astype(BF16), w_ref[...].astype(BF16), preferred_element_type=F32)
            acc = part if acc is None else acc + part
    if has_gain:
        acc = _rms(acc) * refs[2 * n_parts][...]
    o_ref[...] = acc.astype(o_ref.dtype)


def _mm(parts, w, n_out, *, lead=(), tm=1024, tn=512, out_dtype=F32, gain=None, vmem_mib=56):
    t = parts[0][0].shape[0]
    kp = w.shape[-2] // len(parts)
    tm, tn = _fit(t, tm), _fit(n_out, tn)
    if gain is not None:
        assert tn == n_out
    resident = tn == n_out and len(parts) == 1
    nl = (None,) * len(lead)
    w_mode = dict(pipeline_mode=pl.Buffered(1)) if resident else {}
    in_specs = [pl.BlockSpec((tm, kp), functools.partial(lambda i, j, cb: (i, cb), cb=cb)) for _, cb in parts]
    in_specs += [pl.BlockSpec(nl + (kp, tn), functools.partial(lambda i, j, p: lead + (p, j), p=p), **w_mode)
                 for p in range(len(parts))]
    args = [a for a, _ in parts] + [w] * len(parts)
    if gain is not None:
        args.append(gain.reshape(1, n_out))
        in_specs.append(pl.BlockSpec((1, n_out), lambda i, j: (0, 0)))
    return pl.pallas_call(
        functools.partial(_mm_kernel, n_parts=len(parts), has_gain=gain is not None, resident=resident),
        grid=(t // tm, n_out // tn),
        in_specs=in_specs,
        out_specs=pl.BlockSpec((tm, tn), lambda i, j: (i, j)),
        out_shape=jax.ShapeDtypeStruct((t, n_out), out_dtype),
        scratch_shapes=[pltpu.VMEM((kp, tn), BF16)] if resident else [],
        compiler_params=_params(("arbitrary", "arbitrary"), vmem_mib),
        name="proj",
    )(*args)


def _pool_kernel(a_ref, prev_ref, next_ref, w_ref, s_ref, o_ref, pad_ref, *, lay):
    tr = ROW_TILE
    grp = a_ref.shape[1] // len(POOL_WINDOWS)
    i = pl.program_id(0)
    per = lay.l_lat // tr
    is_lat = i < lay.nb_lat * per
    k = i % per
    has_prev = jnp.logical_and(is_lat, k != 0)
    has_next = jnp.logical_and(is_lat, k != per - 1)
    pad_ref[0:POOL_HALO, :] = jnp.where(has_prev, prev_ref[...], 0.0)
    pad_ref[POOL_HALO:POOL_HALO + tr, :] = a_ref[...]
    pad_ref[POOL_HALO + tr:, :] = jnp.where(has_next, next_ref[...], 0.0)
    t = jnp.where(is_lat, k * tr, 0) + lax.broadcasted_iota(jnp.int32, (tr, 1), 0)
    seq_len = jnp.where(is_lat, lay.l_lat, lay.l_ctx)
    for g, win in enumerate(POOL_WINDOWS):
        cols = slice(g * grp, (g + 1) * grp)
        acc = None
        for dlt in range(-(win // 2), win // 2):
            v = pad_ref[POOL_HALO + dlt:POOL_HALO + dlt + tr, cols]
            acc = v if acc is None else acc + v
        cnt = (jnp.clip(t + win // 2, 0, seq_len) - jnp.clip(t - win // 2, 0, seq_len)).astype(F32)
        p = acc / cnt - a_ref[:, cols]
        y = jnp.dot(p.astype(BF16), w_ref[g].astype(BF16), preferred_element_type=F32) * s_ref[:, cols]
        o_ref[:, cols] = y.astype(o_ref.dtype)


def _pool_mixer(lay, a, w_pool, pool_scale, e):
    t, width = a.shape
    tr, hb = ROW_TILE, ROW_TILE // POOL_HALO
    n_halo = t // POOL_HALO
    grp = width // len(POOL_WINDOWS)
    return pl.pallas_call(
        functools.partial(_pool_kernel, lay=lay),
        grid=(t // tr,),
        in_specs=[pl.BlockSpec((tr, width), lambda i: (i, 0)),
                  pl.BlockSpec((POOL_HALO, width), lambda i: (jnp.maximum(i * hb - 1, 0), 0)),
                  pl.BlockSpec((POOL_HALO, width), lambda i: (jnp.minimum((i + 1) * hb, n_halo - 1), 0)),
                  pl.BlockSpec((None, len(POOL_WINDOWS), grp, grp), lambda i: (e, 0, 0, 0)),
                  pl.BlockSpec((None, 1, width), lambda i: (e, 0, 0))],
        out_specs=pl.BlockSpec((tr, width), lambda i: (i, 0)),
        out_shape=jax.ShapeDtypeStruct((t, width), BF16),
        scratch_shapes=[pltpu.VMEM((tr + 2 * POOL_HALO, width), F32)],
        compiler_params=_params(("arbitrary",), 32),
        name="pool_mixer",
    )(a, a, a, w_pool, pool_scale.reshape(pool_scale.shape[0], 1, width))


def _rope_swap(x):
    lane = lax.broadcasted_iota(jnp.int32, (1, x.shape[-1]), 1)
    half = QK_ROPE // 4
    return jnp.where(lane % (2 * half) < half, pltpu.roll(x, x.shape[-1] - half, 1), pltpu.roll(x, half, 1))


def _attn_kernel(*refs, rope, scale, pairs):
    qn_ref, qp_ref, kv_ref, kpe_ref = refs[:4]
    o_ref = refs[-1]
    kpe = kpe_ref[...]
    if rope:
        cq_ref, sq_ref, ck_ref, sk_ref = refs[4:8]
        kpe = kpe * ck_ref[...] + _rope_swap(kpe) * sk_ref[...]
    kpe = kpe.astype(BF16)
    lane = lax.broadcasted_iota(jnp.int32, (1, 2 * QK_ROPE), 1)
    qs = []
    for pair in range(pairs):
        qp = qp_ref[:, pair * 2 * QK_ROPE:(pair + 1) * 2 * QK_ROPE]
        if rope:
            qp = qp * cq_ref[...] + _rope_swap(qp) * sq_ref[...]
        for hh in range(2):
            head = 2 * pair + hh
            own = jnp.logical_and(lane >= hh * QK_ROPE, lane < (hh + 1) * QK_ROPE)
            qs.append(jnp.concatenate([qn_ref[:, head * QK_NOPE:(head + 1) * QK_NOPE], jnp.where(own, qp, 0.0)],
                                      axis=-1).astype(BF16))
    scores = []
    for head, q in enumerate(qs):
        base = head * (QK_NOPE + V_HEAD)
        k = jnp.concatenate([kv_ref[:, base:base + QK_NOPE], kpe], axis=-1)
        scores.append(lax.dot_general(q, k, (((1,), (1,)), ((), ())), preferred_element_type=F32))
    probs = []
    for s in scores:
        p = jnp.exp2((s - jnp.max(s, axis=-1, keepdims=True)) * (scale * LOG2_E))
        probs.append((p * (1.0 / jnp.sum(p, axis=-1, keepdims=True))).astype(BF16))
    for head, p in enumerate(probs):
        base = head * (QK_NOPE + V_HEAD)
        o = jnp.dot(p, kv_ref[:, base + QK_NOPE:base + QK_NOPE + V_HEAD], preferred_element_type=F32)
        o_ref[:, head * V_HEAD:(head + 1) * V_HEAD] = o.astype(o_ref.dtype)


def _attention(q, kv, kpe2, n_heads, *, n_seq, lq, lk, q_row0, kv_row0, tq, out_rows, tables=None, prev=None,
               pairs=4):
    while (n_heads // 2) % pairs:
        pairs -= 1
    n_groups = n_heads // (2 * pairs)
    qt = lq // tq
    rope_col0 = n_heads * QK_NOPE // (2 * QK_ROPE * pairs)
    scale = float(QK_NOPE + QK_ROPE) ** -0.5

    def q_row(b, qi):
        return q_row0 // tq + b * qt + qi

    in_specs = [pl.BlockSpec((tq, 2 * QK_NOPE * pairs), lambda b, g, qi: (q_row(b, qi), g)),
                pl.BlockSpec((tq, 2 * QK_ROPE * pairs), lambda b, g, qi: (q_row(b, qi), rope_col0 + g)),
                pl.BlockSpec((lk, 2 * (QK_NOPE + V_HEAD) * pairs), lambda b, g, qi: (kv_row0 // lk + b, g)),
                pl.BlockSpec((lk, 2 * QK_ROPE), lambda b, g, qi: (kv_row0 // lk + b, 0))]
    args = [q, q, kv, kpe2]
    if tables is not None:
        cq, sq, ck, sk = tables
        in_specs += [pl.BlockSpec((tq, 2 * QK_ROPE), lambda b, g, qi: (qi, 0)),
                     pl.BlockSpec((tq, 2 * QK_ROPE), lambda b, g, qi: (qi, 0)),
                     pl.BlockSpec((lk, 2 * QK_ROPE), lambda b, g, qi: (0, 0)),
                     pl.BlockSpec((lk, 2 * QK_ROPE), lambda b, g, qi: (0, 0))]
        args += [cq, sq, ck, sk]
    aliases = {}
    if prev is not None:
        aliases = {len(args): 0}
        in_specs.append(pl.BlockSpec(memory_space=pl.ANY))
        args.append(prev)
    return pl.pallas_call(
        functools.partial(_attn_kernel, rope=tables is not None, scale=scale, pairs=pairs),
        grid=(n_seq, n_groups, qt),
        in_specs=in_specs,
        out_specs=pl.BlockSpec((tq, 2 * V_HEAD * pairs), lambda b, g, qi: (q_row(b, qi), g)),
        out_shape=jax.ShapeDtypeStruct((out_rows, n_heads * V_HEAD), BF16),
        input_output_aliases=aliases,
        compiler_params=_params(("arbitrary", "arbitrary", "arbitrary"), 48),
        name="attention",
    )(*args)


def _rope_tables(l_lat, past):
    t = jnp.arange(l_lat)
    row = (t // GRID_W).astype(F32)
    col = (t % GRID_W).astype(F32)
    axis = QK_ROPE // 2
    inv = 1.0 / (ROPE_BASE ** (jnp.arange(0, axis, 2, dtype=F32) / axis))
    ar, ac = row[:, None] * inv, col[:, None] * inv
    cos = jnp.concatenate([jnp.cos(ar), jnp.cos(ar), jnp.cos(ac), jnp.cos(ac)], axis=-1)
    sin = jnp.concatenate([-jnp.sin(ar), jnp.sin(ar), -jnp.sin(ac), jnp.sin(ac)], axis=-1)
    cq, sq = jnp.tile(cos, (1, 2)), jnp.tile(sin, (1, 2))
    ck = jnp.concatenate([jnp.ones((past, 2 * QK_ROPE), F32), cq], axis=0)
    sk = jnp.concatenate([jnp.zeros((past, 2 * QK_ROPE), F32), sq], axis=0)
    return cq, sq, ck, sk


def _lower_bound(lb, layer):
    rows = [lb[i:i + 1, :] for i in range(lb.shape[0])]
    m = functools.reduce(jnp.maximum, rows)
    e = [jnp.exp(r - m) for r in rows]
    total = functools.reduce(lambda a, b: a + b, e)
    p = [x / total for x in e]
    cum = functools.reduce(lambda a, b: a + b, p[:layer + 1])
    return cum - p[0]


def _cumsum_rows(x, tri):
    hi = x.astype(BF16)
    r1 = x - hi.astype(F32)
    mid = r1.astype(BF16)
    lo = (r1 - mid.astype(F32)).astype(BF16)
    return (jnp.dot(tri, hi, preferred_element_type=F32) + jnp.dot(tri, mid, preferred_element_type=F32)
            + jnp.dot(tri, lo, preferred_element_type=F32))


def _gla_kernel(*refs, layer, n_chunks, hg, has_state_in, has_state_out, n_prev):
    it = iter(refs)
    zq_ref, zf_ref, zb_ref, zi_ref, zg_ref, lbf_ref, lbb_ref, gain_ref = [next(it) for _ in range(8)]
    s_in_ref = next(it) if has_state_in else None
    for _ in range(n_prev):
        next(it)
    r_ref = next(it)
    s_out_ref = next(it) if has_state_out else None
    of_ref, ob_ref, st_ref, q_ref = next(it), next(it), next(it), next(it)
    q_ref[...] = _silu(zq_ref[...])

    grp = GLA_UNROLL * CHUNK
    row = lax.broadcasted_iota(jnp.int32, (grp, grp), 0)
    col = lax.broadcasted_iota(jnp.int32, (grp, grp), 1)
    same_chunk = (row // CHUNK) == (col // CHUNK)
    masks = (jnp.logical_and(same_chunk, col <= row),
             jnp.logical_and(same_chunk, col >= row))
    tris = tuple(jnp.where(m, 1.0, 0.0).astype(BF16) for m in masks)
    gates = []
    for lb_ref in (lbf_ref, lbb_ref):
        lb = _lower_bound(lb_ref[...], layer)
        gates.append((jnp.log(lb), jnp.log1p(-lb), 1.0 - lb))
    heads = [slice(h * REC_HEAD, (h + 1) * REC_HEAD) for h in range(hg)]
    chunks = [slice(c * CHUNK, (c + 1) * CHUNK) for c in range(GLA_UNROLL)]
    contract_lanes = (((1,), (1,)), ((), ()))
    contract_rows = (((0,), (0,)), ((), ()))

    for dr in range(2):
        for h in range(hg):
            st_ref[dr, h] = s_in_ref[dr, h].T if has_state_in else jnp.zeros((REC_HEAD, REC_HEAD), F32)

    def pair(g_fwd, g_bwd):
        r0s = [g * grp if isinstance(g, int) else pl.multiple_of(g * grp, grp) for g in (g_fwd, g_bwd)]
        log_fs, ks = [], []
        for rev in range(2):
            log_lb, log_1m_lb, one_m_lb = gates[rev]
            z = (zb_ref if rev else zf_ref)[pl.ds(r0s[rev], grp), :]
            t = jnp.exp(-jnp.abs(z))
            one_p_t = 1.0 + t
            log_sig = jnp.minimum(z, 0.0) - jnp.log(one_p_t)
            u = log_1m_lb + log_sig
            log_fs.append(jnp.maximum(log_lb, u) + jnp.log(1.0 + jnp.exp(-jnp.abs(log_lb - u))))
            inv = 1.0 / one_p_t
            ks.append(one_m_lb * jnp.where(z > 0.0, t * inv, inv))
        bs = [_cumsum_rows(log_fs[rev], tris[rev]) for rev in range(2)]

        q_intra, k_intra, q_in, k_out, vs, decays = [], [], [], [], [], []
        for rev in range(2):
            b, k = bs[rev], ks[rev]
            q = q_ref[pl.ds(r0s[rev], grp), :]
            mid_row = CHUNK // 2 if rev else CHUNK // 2 - 1
            last_row = 0 if rev else CHUNK - 1
            mids = [b[c * CHUNK + mid_row:c * CHUNK + mid_row + 1] for c in range(GLA_UNROLL)]
            lasts = [b[c * CHUNK + last_row:c * CHUNK + last_row + 1] for c in range(GLA_UNROLL)]
            b_mid = jnp.concatenate([jnp.broadcast_to(m, (CHUNK, m.shape[1])) for m in mids], axis=0)
            b_last = jnp.concatenate([jnp.broadcast_to(m, (CHUNK, m.shape[1])) for m in lasts], axis=0)
            q_intra.append((q * jnp.exp(b - b_mid)).astype(BF16))
            k_intra.append((k * jnp.exp(b_mid - b)).astype(BF16))
            q_in.append((q * jnp.exp(b)).astype(BF16))
            k_out.append((k * jnp.exp(b_last - b)).astype(BF16))
            vs.append(zi_ref[pl.ds(r0s[rev], grp), :].astype(BF16))
            decays.append([jnp.exp(m) for m in lasts])

        keys = [(rev, h) for rev in range(2) for h in range(hg)]
        scores = {(rev, h): lax.dot_general(q_intra[rev][:, heads[h]], k_intra[rev][:, heads[h]], contract_lanes,
                                            preferred_element_type=F32) for rev, h in keys}
        scores = {(rev, h): jnp.where(masks[rev], scores[rev, h], 0.0).astype(BF16) for rev, h in keys}
        o_intra = {(rev, h): jnp.dot(scores[rev, h], vs[rev][:, heads[h]], preferred_element_type=F32)
                   for rev, h in keys}
        kv = {(rev, h, c): lax.dot_general(vs[rev][chunks[c], heads[h]], k_out[rev][chunks[c], heads[h]],
                                           contract_rows, preferred_element_type=F32)
              for rev, h in keys for c in range(GLA_UNROLL)}
        st = {(rev, h): st_ref[rev, h] for rev, h in keys}
        for step in range(GLA_UNROLL):
            cs = [step, GLA_UNROLL - 1 - step]
            inter = {(rev, h): lax.dot_general(q_in[rev][chunks[cs[rev]], heads[h]], st[rev, h].astype(BF16),
                                               contract_lanes, preferred_element_type=F32) for rev, h in keys}
            for rev, h in keys:
                c = cs[rev]
                o_dst = ob_ref if rev else of_ref
                o_dst[pl.ds(r0s[rev] + c * CHUNK, CHUNK), heads[h]] = o_intra[rev, h][chunks[c]] + inter[rev, h]
                st[rev, h] = st[rev, h] * decays[rev][c][:, heads[h]] + kv[rev, h, c]
        for rev, h in keys:
            st_ref[rev, h] = st[rev, h]

    n_groups = n_chunks // GLA_UNROLL
    if n_groups == 1:
        pair(0, 0)
    else:
        def body(i, carry):
            pair(i, n_groups - 1 - i)
            return carry
        lax.fori_loop(0, n_groups, body, 0)

    gate = _silu(zg_ref[...])
    o = of_ref[...] + ob_ref[...]
    for h in range(hg):
        sl = slice(h * REC_HEAD, (h + 1) * REC_HEAD)
        r_ref[:, sl] = (_rms(o[:, sl]) * gain_ref[...] * gate[:, sl]).astype(r_ref.dtype)
    if has_state_out:
        for dr in range(2):
            for h in range(hg):
                s_out_ref[dr, h] = st_ref[dr, h].T


def _gla(z, lb_fwd, lb_bwd, g_gain, layer, *, n_seq, seq_len, row0, odd, n_odd, state_in=None, want_state=False,
         prev_r=None, prev_state=None, hg=4):
    width = z.shape[1] // 5
    n_heads = width // REC_HEAD
    w = hg * REC_HEAD
    ncb = width // w
    n_chunks = seq_len // CHUNK
    assert n_chunks % GLA_UNROLL == 0

    def z_spec(part):
        return pl.BlockSpec((seq_len, w), lambda s, g: (row0 // seq_len + s, part * ncb + g))

    in_specs = [z_spec(p) for p in range(5)]
    in_specs += [pl.BlockSpec((lb_fwd.shape[0], w), lambda s, g: (0, g)),
                 pl.BlockSpec((lb_bwd.shape[0], w), lambda s, g: (0, g)),
                 pl.BlockSpec((1, REC_HEAD), lambda s, g: (0, 0))]
    args = [z] * 5 + [lb_fwd, lb_bwd, g_gain.reshape(1, REC_HEAD)]
    if state_in is not None:
        in_specs.append(pl.BlockSpec((None, None, 2, hg, REC_HEAD, REC_HEAD), lambda s, g: (s, odd, 0, g, 0, 0)))
        args.append(state_in)
    out_shape = [jax.ShapeDtypeStruct((z.shape[0], width), BF16)]
    out_specs = [pl.BlockSpec((seq_len, w), lambda s, g: (row0 // seq_len + s, g))]
    if want_state:
        out_shape.append(jax.ShapeDtypeStruct((n_seq, n_odd, 2, n_heads, REC_HEAD, REC_HEAD), F32))
        out_specs.append(pl.BlockSpec((None, None, 2, hg, REC_HEAD, REC_HEAD), lambda s, g: (s, odd, 0, g, 0, 0)))
    aliases = {}
    for out_idx, prev in enumerate((prev_r, prev_state)):
        if prev is not None:
            aliases[len(args)] = out_idx
            in_specs.append(pl.BlockSpec(memory_space=pl.ANY))
            args.append(prev)
    return pl.pallas_call(
        functools.partial(_gla_kernel, layer=layer, n_chunks=n_chunks, hg=hg,
                          has_state_in=state_in is not None, has_state_out=want_state, n_prev=len(aliases)),
        grid=(n_seq, ncb),
        in_specs=in_specs, out_specs=out_specs, out_shape=out_shape,
        input_output_aliases=aliases,
        scratch_shapes=[pltpu.VMEM((seq_len, w), F32), pltpu.VMEM((seq_len, w), F32),
                        pltpu.VMEM((2, hg, REC_HEAD, REC_HEAD), F32), pltpu.VMEM((seq_len, w), F32)],
        compiler_params=_params(("arbitrary", "arbitrary"), 48),
        name="hgrn2",
    )(*args)


def _even_mixer(lay, h, e, cache_ckv, cache_kpe, w_in_even, q_a_gain, kv_a_gain, w_qb, w_kvb, w_pool,
                pool_scale, w_out_even):
    pool_width = pool_scale.shape[1]
    q_lora, kv_lora = q_a_gain.shape[1], kv_a_gain.shape[1]
    n_heads = w_qb.shape[2] // (QK_NOPE + QK_ROPE)
    past = cache_ckv.shape[2]
    o1, o2, o3 = pool_width, pool_width + q_lora, pool_width + q_lora + kv_lora

    a = _mm([(h, 0)], w_in_even, pool_width, lead=(e,), tm=512, tn=pool_width)
    cq = _mm([(h, 0)], w_in_even[e, :, o1:o2], q_lora, tn=q_lora, out_dtype=BF16, gain=q_a_gain[e])
    ckv = _mm([(h, 0)], w_in_even[e, :, o2:o3], kv_lora, tn=kv_lora, gain=kv_a_gain[e])
    kpe = _mm([(h, 0)], w_in_even[e, :, o3:], QK_ROPE)

    wq = w_qb[e].reshape(q_lora, n_heads, QK_NOPE + QK_ROPE)
    wq = jnp.concatenate([wq[:, :, :QK_NOPE].reshape(q_lora, -1), wq[:, :, QK_NOPE:].reshape(q_lora, -1)], axis=1)
    q = _mm([(cq, 0)], wq, wq.shape[1], tn=1536)

    def with_cache(cache, new):
        lat = jnp.concatenate([cache, new[:lay.t_lat].reshape(lay.nb_lat, lay.l_lat, -1)], axis=1)
        return jnp.concatenate([lat.reshape(lay.nb_lat * (past + lay.l_lat), -1), new[lay.t_lat:]], axis=0)

    ckv_all = with_cache(cache_ckv[:, e], ckv)
    kpe_all = jnp.tile(with_cache(cache_kpe[:, e], kpe), (1, 2))
    kv = _mm([(ckv_all, 0)], w_kvb, w_kvb.shape[2], lead=(e,), tm=1664, tn=1024, out_dtype=BF16)

    lk_lat = past + lay.l_lat
    att = _attention(q, kv, kpe_all, n_heads, n_seq=lay.nb_lat, lq=lay.l_lat, lk=lk_lat, q_row0=0, kv_row0=0,
                     tq=ROW_TILE, out_rows=lay.t, tables=_rope_tables(lay.l_lat, past))
    att = _attention(q, kv, kpe_all, n_heads, n_seq=lay.nb_ctx, lq=lay.l_ctx, lk=lay.l_ctx, q_row0=lay.t_lat,
                     kv_row0=lay.nb_lat * lk_lat, tq=lay.l_ctx, out_rows=lay.t, prev=att)
    pooled = _pool_mixer(lay, a, w_pool, pool_scale, e)

    parts = [(pooled, 0)] + [(att, cb) for cb in range(att.shape[1] // pool_width)]
    y = _mm(parts, w_out_even, w_out_even.shape[2], lead=(e,))
    return y, ckv[lay.t_lat:], kpe[lay.t_lat:]


def _odd_mixer(lay, h, o, layer, state_hgrn, prev_states, w_in_odd, lb_fwd, lb_bwd, g_norm_gain, w_out_odd):
    n_odd = w_in_odd.shape[0]
    z = _mm([(h, 0)], w_in_odd, w_in_odd.shape[2], lead=(o,))
    (r,) = _gla(z, lb_fwd, lb_bwd, g_norm_gain[o], layer, n_seq=lay.nb_lat, seq_len=lay.l_lat, row0=0,
                odd=o, n_odd=n_odd, state_in=state_hgrn)
    r, states = _gla(z, lb_fwd, lb_bwd, g_norm_gain[o], layer, n_seq=lay.nb_ctx, seq_len=lay.l_ctx,
                     row0=lay.t_lat, odd=o, n_odd=n_odd, want_state=True, prev_r=r, prev_state=prev_states)
    return _mm([(r, 0)], w_out_odd, w_out_odd.shape[2], lead=(o,)), states


def kernel(x_prompt, x_sample, cache_ckv, cache_kpe, state_hgrn, c, c_ctx, w_ada, b_ada, ffn_w1, ffn_w3, ffn_w2, w_in_even, q_a_gain, kv_a_gain, w_qb, w_kvb, w_pool, pool_scale, w_out_even, w_in_odd, lb_fwd, lb_bwd, g_norm_gain, w_out_odd, final_gain):
    nb_ctx, l_ctx, d = x_prompt.shape
    nb_lat, l_lat, _ = x_sample.shape
    depth = w_ada.shape[0]
    lay = _Layout(nb_lat, l_lat, nb_ctx, l_ctx)
    assert nb_lat + 1 <= MOD_ROWS and l_lat % ROW_TILE == 0 and l_ctx == ROW_TILE
    assert (nb_lat * (cache_ckv.shape[2] + l_lat)) % l_ctx == 0

    x = jnp.concatenate([x_sample.reshape(lay.t_lat, d), x_prompt.reshape(lay.t_ctx, d)], axis=0)
    cmat = jnp.concatenate([c, c_ctx[None, :], jnp.zeros((MOD_ROWS - nb_lat - 1, d), F32)], axis=0)
    n_mod_cols = w_ada.shape[2]
    b_ada3 = b_ada.reshape(depth, 1, n_mod_cols)
    t_ffn = _fit(lay.t, 1024)
    side_tn = _side_tile(n_mod_cols // 2, (lay.t // t_ffn) * (ffn_w1.shape[-1] // 256))

    def ffn(h, layer, half):
        if layer + 1 == depth or side_tn is None:
            return _ffn(h, ffn_w1, ffn_w3, ffn_w2, layer, half), None
        side = (cmat, w_ada, b_ada3, layer + 1, half * (n_mod_cols // 2), n_mod_cols // 2, side_tn)
        return _ffn(h, ffn_w1, ffn_w3, ffn_w2, layer, half, side=side)

    def as_mods(flat):
        return flat.reshape(MOD_ROWS, N_MOD, d)

    mods = [as_mods(_ada_mods(cmat, w_ada, b_ada3, 0))]

    new_ckv, new_kpe, new_states = [], [], None
    x, h = _transition(lay, mods, x, mod=(0, 0))
    for layer in range(depth):
        y, mods_lo = ffn(h, layer, 0)
        x, h = _transition(lay, mods, x, y, gate=(layer, 2, 0.5), mod=(layer, 3))
        if layer % 2 == 0:
            y, ckv, kpe = _even_mixer(lay, h, layer // 2, cache_ckv, cache_kpe, w_in_even, q_a_gain, kv_a_gain,
                                      w_qb, w_kvb, w_pool, pool_scale, w_out_even)
            new_ckv.append(ckv.reshape(nb_ctx, l_ctx, -1))
            new_kpe.append(kpe.reshape(nb_ctx, l_ctx, -1))
        else:
            y, new_states = _odd_mixer(lay, h, layer // 2, layer, state_hgrn, new_states, w_in_odd, lb_fwd,
                                       lb_bwd, g_norm_gain, w_out_odd)
        x, h = _transition(lay, mods, x, y, gate=(layer, 5, 1.0), mod=(layer, 6))
        y, mods_hi = ffn(h, layer, 1)
        if layer + 1 < depth:
            if mods_lo is None:
                mods.append(as_mods(_ada_mods(cmat, w_ada, b_ada3, layer + 1)))
            else:
                mods.append(as_mods(jnp.concatenate([mods_lo, mods_hi], axis=1)))
            x, h = _transition(lay, mods, x, y, gate=(layer, 8, 0.5), mod=(layer + 1, 0))
    gate = (depth - 1, 8, 0.5)
    lat_tiles = lay.t_lat // ROW_TILE
    y_sample = _transition(lay, mods, x, y, gate=gate, final_gain=final_gain, rows=(0, lat_tiles))
    y_prompt = _transition(lay, mods, x, y, gate=gate, final_gain=final_gain,
                           rows=(lat_tiles, lay.t_ctx // ROW_TILE))
    return (y_prompt.reshape(nb_ctx, l_ctx, d), y_sample.reshape(nb_lat, l_lat, d),
            jnp.stack(new_ckv, axis=1), jnp.stack(new_kpe, axis=1), new_states)
```

```python
import functools

import jax
import jax.numpy as jnp
from jax import lax
from jax.experimental import pallas as pl
from jax.experimental.pallas import tpu as pltpu

F32 = jnp.float32
BF16 = jnp.bfloat16

EPS = 1e-6
N_MOD = 9
GRID_W = 64
ROPE_BASE = 10000.0
POOL_WINDOWS = (2, 4, 8, 16)
POOL_HALO = 8
QK_NOPE = 128
QK_ROPE = 64
V_HEAD = 128
REC_HEAD = 128
CHUNK = 64
GLA_UNROLL = 4
PACKED_SUBLANES = 16
MXU_COLS = 256
ROW_TILE = 256
MOD_ROWS = 8
MIB = 1 << 20
LOG2_E = 1.4426950408889634


def _params(semantics, vmem_mib):
    return pltpu.CompilerParams(dimension_semantics=semantics, vmem_limit_bytes=vmem_mib * MIB)


class _Layout:
    def __init__(self, nb_lat, l_lat, nb_ctx, l_ctx):
        self.nb_lat, self.l_lat, self.nb_ctx, self.l_ctx = nb_lat, l_lat, nb_ctx, l_ctx
        self.t_lat = nb_lat * l_lat
        self.t_ctx = nb_ctx * l_ctx
        self.t = self.t_lat + self.t_ctx

    def group(self, i, rows):
        per = self.l_lat // rows
        return jnp.where(i < self.nb_lat * per, i // per, self.nb_lat)


def _fit(n, tile):
    if n <= tile:
        return n
    return max(d for d in range(PACKED_SUBLANES, tile + 1, PACKED_SUBLANES) if n % d == 0)


def _silu(x):
    return x * jax.nn.sigmoid(x)


def _rms(x):
    return x * lax.rsqrt(jnp.mean(x * x, axis=-1, keepdims=True) + EPS)


def _ada_kernel(c_ref, w_ref, b_ref, o_ref):
    s = _silu(c_ref[...]).astype(BF16)
    o_ref[...] = jnp.dot(s, w_ref[...].astype(BF16), preferred_element_type=F32) + b_ref[...]


def _ada_mods(cmat, w_ada, b_ada3, layer, tn=512):
    _, d, n = w_ada.shape
    return pl.pallas_call(
        _ada_kernel,
        grid=(n // tn,),
        in_specs=[pl.BlockSpec((MOD_ROWS, d), lambda j: (0, 0)),
                  pl.BlockSpec((None, d, tn), lambda j: (layer, 0, j)),
                  pl.BlockSpec((None, 1, tn), lambda j: (layer, 0, j))],
        out_specs=pl.BlockSpec((MOD_ROWS, tn), lambda j: (0, j)),
        out_shape=jax.ShapeDtypeStruct((MOD_ROWS, n), F32),
        compiler_params=_params(("arbitrary",), 40),
        name="ada_mods",
    )(cmat, w_ada, b_ada3)


def _trans_kernel(*refs, has_y, coef, gate_idx, mod_idx, final):
    it = iter(refs)
    x = next(it)[...]
    if has_y:
        y_ref, mg_ref = next(it), next(it)
        g = mg_ref[gate_idx:gate_idx + 1, :]
        if coef != 1.0:
            g = coef * g
        x = x + g * y_ref[...]
    xn = _rms(x)
    if final:
        gain_ref, o_ref = next(it), next(it)
        o_ref[...] = xn * gain_ref[...]
        return
    mm_ref = next(it)
    shift = mm_ref[mod_idx:mod_idx + 1, :]
    scale = mm_ref[mod_idx + 1:mod_idx + 2, :]
    if has_y:
        next(it)[...] = x
    next(it)[...] = (xn * (1.0 + scale) + shift).astype(BF16)


def _transition(lay, mods, x, y=None, gate=None, mod=None, final_gain=None, rows=None):
    t, d = x.shape
    tr = ROW_TILE
    off, n_tiles = rows if rows is not None else (0, t // tr)
    row_spec = pl.BlockSpec((tr, d), lambda i: (i + off, 0))
    mod_spec = pl.BlockSpec((None, N_MOD, d), lambda i: (lay.group(i + off, tr), 0, 0))

    args, in_specs = [x], [row_spec]
    has_y = y is not None
    coef, gate_idx, mod_idx = 1.0, 0, 0
    if has_y:
        gate_layer, gate_idx, coef = gate
        args += [y, mods[gate_layer]]
        in_specs += [row_spec, mod_spec]
    if final_gain is not None:
        args.append(final_gain.reshape(1, d))
        in_specs.append(pl.BlockSpec((1, d), lambda i: (0, 0)))
        out_shape = jax.ShapeDtypeStruct((n_tiles * tr, d), F32)
        out_specs = pl.BlockSpec((tr, d), lambda i: (i, 0))
    else:
        mod_layer, mod_idx = mod
        args.append(mods[mod_layer])
        in_specs.append(mod_spec)
        out_shape = [jax.ShapeDtypeStruct((t, d), BF16)]
        out_specs = [pl.BlockSpec((tr, d), lambda i: (i, 0))]
        if has_y:
            out_shape.insert(0, jax.ShapeDtypeStruct((t, d), F32))
            out_specs.insert(0, pl.BlockSpec((tr, d), lambda i: (i, 0)))
    out = pl.pallas_call(
        functools.partial(_trans_kernel, has_y=has_y, coef=coef, gate_idx=gate_idx, mod_idx=mod_idx,
                          final=final_gain is not None),
        grid=(n_tiles,),
        in_specs=in_specs, out_specs=out_specs, out_shape=out_shape,
        compiler_params=_params(("arbitrary",), 48),
        name="transition",
    )(*args)
    if final_gain is not None:
        return out
    return (out[0], out[1]) if has_y else (x, out[0])


def _ffn_kernel(*refs, n_side):
    h_ref, w1_ref, w3_ref, w2_ref = refs[:4]
    o_ref = refs[-2] if n_side else refs[-1]

    @pl.when(pl.program_id(1) == 0)
    def _():
        o_ref[...] = jnp.zeros_like(o_ref)

    h = h_ref[...]
    a1 = jnp.dot(h, w1_ref[...].astype(BF16), preferred_element_type=F32)
    a3 = jnp.dot(h, w3_ref[...].astype(BF16), preferred_element_type=F32)
    g = (_silu(a1) * a3).astype(BF16)
    o_ref[...] += jnp.dot(g, w2_ref[...].astype(BF16), preferred_element_type=F32)

    if n_side:
        c_ref, wa_ref, ba_ref = refs[4:7]
        m_ref = refs[-1]

        @pl.when(pl.program_id(0) * pl.num_programs(1) + pl.program_id(1) < n_side)
        def _():
            _ada_kernel(c_ref, wa_ref, ba_ref, m_ref)


def _side_tile(n_cols, steps):
    for tn in range(MXU_COLS, n_cols + 1, MXU_COLS):
        if n_cols % tn == 0 and n_cols // tn <= steps:
            return tn
    return None


def _ffn(h, w1, w3, w2, layer, half, side=None, tm=1024, tf=256):
    t, d = h.shape
    dff = w1.shape[-1]
    tm = _fit(t, tm)
    nj = dff // tf
    in_specs = [pl.BlockSpec((tm, d), lambda i, j: (i, 0), pipeline_mode=pl.Buffered(1)),
                pl.BlockSpec((None, None, d, tf), lambda i, j: (layer, half, 0, j)),
                pl.BlockSpec((None, None, d, tf), lambda i, j: (layer, half, 0, j)),
                pl.BlockSpec((None, None, tf, d), lambda i, j: (layer, half, j, 0))]
    args = [h, w1, w3, w2]
    out_specs = [pl.BlockSpec((tm, d), lambda i, j: (i, 0), pipeline_mode=pl.Buffered(1))]
    out_shape = [jax.ShapeDtypeStruct((t, d), F32)]
    n_side = 0
    if side is not None:
        cmat, w_ada, b_ada3, ada_layer, col0, n_cols, tn = side
        n_side = n_cols // tn

        def tile(i, j):
            return jnp.minimum(i * nj + j, n_side - 1)

        in_specs += [pl.BlockSpec((MOD_ROWS, d), lambda i, j: (0, 0)),
                     pl.BlockSpec((None, d, tn), lambda i, j: (ada_layer, 0, col0 // tn + tile(i, j))),
                     pl.BlockSpec((None, 1, tn), lambda i, j: (ada_layer, 0, col0 // tn + tile(i, j)))]
        args += [cmat, w_ada, b_ada3]
        out_specs.append(pl.BlockSpec((MOD_ROWS, tn), lambda i, j: (0, tile(i, j))))
        out_shape.append(jax.ShapeDtypeStruct((MOD_ROWS, n_cols), F32))
    out = pl.pallas_call(
        functools.partial(_ffn_kernel, n_side=n_side),
        grid=(t // tm, nj),
        in_specs=in_specs, out_specs=out_specs, out_shape=out_shape,
        compiler_params=_params(("arbitrary", "arbitrary"), 60),
        name="ffn",
    )(*args)
    return out if side is not None else out[0]


def _mm_kernel(*refs, n_parts, has_gain, resident):
    x_refs, w_refs = refs[:n_parts], refs[n_parts:2 * n_parts]
    if resident:
        wb_ref = refs[-1]
        o_ref = refs[-2]

        @pl.when(pl.program_id(0) == 0)
        def _():
            wb_ref[...] = w_refs[0][...].astype(BF16)

        acc = jnp.dot(x_refs[0][...].astype(BF16), wb_ref[...], preferred_element_type=F32)
    else:
        o_ref = refs[-1]
        acc = None
        for x_ref, w_ref in zip(x_refs, w_refs):
            part = jnp.dot(x_ref[...].astype(BF16), w_ref[...].astype(BF16), preferred_element_type=F32)
            acc = part if acc is None else acc + part
    if has_gain:
        acc = _rms(acc) * refs[2 * n_parts][...]
    o_ref[...] = acc.astype(o_ref.dtype)


def _mm(parts, w, n_out, *, lead=(), tm=1024, tn=512, out_dtype=F32, gain=None, vmem_mib=56):
    t = parts[0][0].shape[0]
    kp = w.shape[-2] // len(parts)
    tm, tn = _fit(t, tm), _fit(n_out, tn)
    if gain is not None:
        assert tn == n_out
    resident = tn == n_out and len(parts) == 1
    nl = (None,) * len(lead)
    w_mode = dict(pipeline_mode=pl.Buffered(1)) if resident else {}
    in_specs = [pl.BlockSpec((tm, kp), functools.partial(lambda i, j, cb: (i, cb), cb=cb)) for _, cb in parts]
    in_specs += [pl.BlockSpec(nl + (kp, tn), functools.partial(lambda i, j, p: lead + (p, j), p=p), **w_mode)
                 for p in range(len(parts))]
    args = [a for a, _ in parts] + [w] * len(parts)
    if gain is not None:
        args.append(gain.reshape(1, n_out))
        in_specs.append(pl.BlockSpec((1, n_out), lambda i, j: (0, 0)))
    return pl.pallas_call(
        functools.partial(_mm_kernel, n_parts=len(parts), has_gain=gain is not None, resident=resident),
        grid=(t // tm, n_out // tn),
        in_specs=in_specs,
        out_specs=pl.BlockSpec((tm, tn), lambda i, j: (i, j)),
        out_shape=jax.ShapeDtypeStruct((t, n_out), out_dtype),
        scratch_shapes=[pltpu.VMEM((kp, tn), BF16)] if resident else [],
        compiler_params=_params(("arbitrary", "arbitrary"), vmem_mib),
        name="proj",
    )(*args)


def _pool_kernel(a_ref, prev_ref, next_ref, w_ref, s_ref, o_ref, pad_ref, *, lay):
    tr = ROW_TILE
    grp = a_ref.shape[1] // len(POOL_WINDOWS)
    i = pl.program_id(0)
    per = lay.l_lat // tr
    is_lat = i < lay.nb_lat * per
    k = i % per
    has_prev = jnp.logical_and(is_lat, k != 0)
    has_next = jnp.logical_and(is_lat, k != per - 1)
    pad_ref[0:POOL_HALO, :] = jnp.where(has_prev, prev_ref[...], 0.0)
    pad_ref[POOL_HALO:POOL_HALO + tr, :] = a_ref[...]
    pad_ref[POOL_HALO + tr:, :] = jnp.where(has_next, next_ref[...], 0.0)
    t = jnp.where(is_lat, k * tr, 0) + lax.broadcasted_iota(jnp.int32, (tr, 1), 0)
    seq_len = jnp.where(is_lat, lay.l_lat, lay.l_ctx)
    for g, win in enumerate(POOL_WINDOWS):
        cols = slice(g * grp, (g + 1) * grp)
        acc = None
        for dlt in range(-(win // 2), win // 2):
            v = pad_ref[POOL_HALO + dlt:POOL_HALO + dlt + tr, cols]
            acc = v if acc is None else acc + v
        cnt = (jnp.clip(t + win // 2, 0, seq_len) - jnp.clip(t - win // 2, 0, seq_len)).astype(F32)
        p = acc / cnt - a_ref[:, cols]
        y = jnp.dot(p.astype(BF16), w_ref[g].astype(BF16), preferred_element_type=F32) * s_ref[:, cols]
        o_ref[:, cols] = y.astype(o_ref.dtype)


def _pool_mixer(lay, a, w_pool, pool_scale, e):
    t, width = a.shape
    tr, hb = ROW_TILE, ROW_TILE // POOL_HALO
    n_halo = t // POOL_HALO
    grp = width // len(POOL_WINDOWS)
    return pl.pallas_call(
        functools.partial(_pool_kernel, lay=lay),
        grid=(t // tr,),
        in_specs=[pl.BlockSpec((tr, width), lambda i: (i, 0)),
                  pl.BlockSpec((POOL_HALO, width), lambda i: (jnp.maximum(i * hb - 1, 0), 0)),
                  pl.BlockSpec((POOL_HALO, width), lambda i: (jnp.minimum((i + 1) * hb, n_halo - 1), 0)),
                  pl.BlockSpec((None, len(POOL_WINDOWS), grp, grp), lambda i: (e, 0, 0, 0)),
                  pl.BlockSpec((None, 1, width), lambda i: (e, 0, 0))],
        out_specs=pl.BlockSpec((tr, width), lambda i: (i, 0)),
        out_shape=jax.ShapeDtypeStruct((t, width), BF16),
        scratch_shapes=[pltpu.VMEM((tr + 2 * POOL_HALO, width), F32)],
        compiler_params=_params(("arbitrary",), 32),
        name="pool_mixer",
    )(a, a, a, w_pool, pool_scale.reshape(pool_scale.shape[0], 1, width))


def _rope_swap(x):
    lane = lax.broadcasted_iota(jnp.int32, (1, x.shape[-1]), 1)
    half = QK_ROPE // 4
    return jnp.where(lane % (2 * half) < half, pltpu.roll(x, x.shape[-1] - half, 1), pltpu.roll(x, half, 1))


def _attn_kernel(*refs, rope, scale, pairs):
    qn_ref, qp_ref, kv_ref, kpe_ref = refs[:4]
    o_ref = refs[-1]
    kpe = kpe_ref[...]
    if rope:
        cq_ref, sq_ref, ck_ref, sk_ref = refs[4:8]
        kpe = kpe * ck_ref[...] + _rope_swap(kpe) * sk_ref[...]
    kpe = kpe.astype(BF16)
    lane = lax.broadcasted_iota(jnp.int32, (1, 2 * QK_ROPE), 1)
    qs = []
    for pair in range(pairs):
        qp = qp_ref[:, pair * 2 * QK_ROPE:(pair + 1) * 2 * QK_ROPE]
        if rope:
            qp = qp * cq_ref[...] + _rope_swap(qp) * sq_ref[...]
        for hh in range(2):
            head = 2 * pair + hh
            own = jnp.logical_and(lane >= hh * QK_ROPE, lane < (hh + 1) * QK_ROPE)
            qs.append(jnp.concatenate([qn_ref[:, head * QK_NOPE:(head + 1) * QK_NOPE], jnp.where(own, qp, 0.0)],
                                      axis=-1).astype(BF16))
    scores = []
    for head, q in enumerate(qs):
        base = head * (QK_NOPE + V_HEAD)
        k = jnp.concatenate([kv_ref[:, base:base + QK_NOPE], kpe], axis=-1)
        scores.append(lax.dot_general(q, k, (((1,), (1,)), ((), ())), preferred_element_type=F32))
    probs = []
    for s in scores:
        p = jnp.exp2((s - jnp.max(s, axis=-1, keepdims=True)) * (scale * LOG2_E))
        probs.append((p * (1.0 / jnp.sum(p, axis=-1, keepdims=True))).astype(BF16))
    for head, p in enumerate(probs):
        base = head * (QK_NOPE + V_HEAD)
        o = jnp.dot(p, kv_ref[:, base + QK_NOPE:base + QK_NOPE + V_HEAD], preferred_element_type=F32)
        o_ref[:, head * V_HEAD:(head + 1) * V_HEAD] = o.astype(o_ref.dtype)


def _attention(q, kv, kpe2, n_heads, *, n_seq, lq, lk, q_row0, kv_row0, tq, out_rows, tables=None, prev=None,
               pairs=4):
    while (n_heads // 2) % pairs:
        pairs -= 1
    n_groups = n_heads // (2 * pairs)
    qt = lq // tq
    rope_col0 = n_heads * QK_NOPE // (2 * QK_ROPE * pairs)
    scale = float(QK_NOPE + QK_ROPE) ** -0.5

    def q_row(b, qi):
        return q_row0 // tq + b * qt + qi

    in_specs = [pl.BlockSpec((tq, 2 * QK_NOPE * pairs), lambda b, g, qi: (q_row(b, qi), g)),
                pl.BlockSpec((tq, 2 * QK_ROPE * pairs), lambda b, g, qi: (q_row(b, qi), rope_col0 + g)),
                pl.BlockSpec((lk, 2 * (QK_NOPE + V_HEAD) * pairs), lambda b, g, qi: (kv_row0 // lk + b, g)),
                pl.BlockSpec((lk, 2 * QK_ROPE), lambda b, g, qi: (kv_row0 // lk + b, 0))]
    args = [q, q, kv, kpe2]
    if tables is not None:
        cq, sq, ck, sk = tables
        in_specs += [pl.BlockSpec((tq, 2 * QK_ROPE), lambda b, g, qi: (qi, 0)),
                     pl.BlockSpec((tq, 2 * QK_ROPE), lambda b, g, qi: (qi, 0)),
                     pl.BlockSpec((lk, 2 * QK_ROPE), lambda b, g, qi: (0, 0)),
                     pl.BlockSpec((lk, 2 * QK_ROPE), lambda b, g, qi: (0, 0))]
        args += [cq, sq, ck, sk]
    aliases = {}
    if prev is not None:
        aliases = {len(args): 0}
        in_specs.append(pl.BlockSpec(memory_space=pl.ANY))
        args.append(prev)
    return pl.pallas_call(
        functools.partial(_attn_kernel, rope=tables is not None, scale=scale, pairs=pairs),
        grid=(n_seq, n_groups, qt),
        in_specs=in_specs,
        out_specs=pl.BlockSpec((tq, 2 * V_HEAD * pairs), lambda b, g, qi: (q_row(b, qi), g)),
        out_shape=jax.ShapeDtypeStruct((out_rows, n_heads * V_HEAD), BF16),
        input_output_aliases=aliases,
        compiler_params=_params(("arbitrary", "arbitrary", "arbitrary"), 48),
        name="attention",
    )(*args)


def _rope_tables(l_lat, past):
    t = jnp.arange(l_lat)
    row = (t // GRID_W).astype(F32)
    col = (t % GRID_W).astype(F32)
    axis = QK_ROPE // 2
    inv = 1.0 / (ROPE_BASE ** (jnp.arange(0, axis, 2, dtype=F32) / axis))
    ar, ac = row[:, None] * inv, col[:, None] * inv
    cos = jnp.concatenate([jnp.cos(ar), jnp.cos(ar), jnp.cos(ac), jnp.cos(ac)], axis=-1)
    sin = jnp.concatenate([-jnp.sin(ar), jnp.sin(ar), -jnp.sin(ac), jnp.sin(ac)], axis=-1)
    cq, sq = jnp.tile(cos, (1, 2)), jnp.tile(sin, (1, 2))
    ck = jnp.concatenate([jnp.ones((past, 2 * QK_ROPE), F32), cq], axis=0)
    sk = jnp.concatenate([jnp.zeros((past, 2 * QK_ROPE), F32), sq], axis=0)
    return cq, sq, ck, sk


def _lower_bound(lb, layer):
    rows = [lb[i:i + 1, :] for i in range(lb.shape[0])]
    m = functools.reduce(jnp.maximum, rows)
    e = [jnp.exp(r - m) for r in rows]
    total = functools.reduce(lambda a, b: a + b, e)
    p = [x / total for x in e]
    cum = functools.reduce(lambda a, b: a + b, p[:layer + 1])
    return cum - p[0]


def _cumsum_rows(x, tri):
    hi = x.astype(BF16)
    r1 = x - hi.astype(F32)
    mid = r1.astype(BF16)
    lo = (r1 - mid.astype(F32)).astype(BF16)
    return (jnp.dot(tri, hi, preferred_element_type=F32) + jnp.dot(tri, mid, preferred_element_type=F32)
            + jnp.dot(tri, lo, preferred_element_type=F32))


def _gla_kernel(*refs, layer, n_chunks, hg, has_state_in, has_state_out, n_prev, odd, zero_odd):
    it = iter(refs)
    zq_ref, zf_ref, zb_ref, zi_ref, zg_ref, lbf_ref, lbb_ref, gain_ref = [next(it) for _ in range(8)]
    s_in_ref = next(it) if has_state_in else None
    for _ in range(n_prev):
        next(it)
    r_ref = next(it)
    s_out_ref = next(it) if has_state_out else None
    of_ref, ob_ref, st_ref, q_ref = next(it), next(it), next(it), next(it)
    q_ref[...] = _silu(zq_ref[...])

    grp = GLA_UNROLL * CHUNK
    row = lax.broadcasted_iota(jnp.int32, (grp, grp), 0)
    col = lax.broadcasted_iota(jnp.int32, (grp, grp), 1)
    same_chunk = (row // CHUNK) == (col // CHUNK)
    masks = (jnp.logical_and(same_chunk, col <= row),
             jnp.logical_and(same_chunk, col >= row))
    tris = tuple(jnp.where(m, 1.0, 0.0).astype(BF16) for m in masks)
    gates = []
    for lb_ref in (lbf_ref, lbb_ref):
        lb = _lower_bound(lb_ref[...], layer)
        gates.append((jnp.log(lb), jnp.log1p(-lb), 1.0 - lb))
    heads = [slice(h * REC_HEAD, (h + 1) * REC_HEAD) for h in range(hg)]
    chunks = [slice(c * CHUNK, (c + 1) * CHUNK) for c in range(GLA_UNROLL)]
    contract_lanes = (((1,), (1,)), ((), ()))
    contract_rows = (((0,), (0,)), ((), ()))

    for dr in range(2):
        for h in range(hg):
            st_ref[dr, h] = s_in_ref[dr, h].T if has_state_in else jnp.zeros((REC_HEAD, REC_HEAD), F32)

    def pair(g_fwd, g_bwd):
        r0s = [g * grp if isinstance(g, int) else pl.multiple_of(g * grp, grp) for g in (g_fwd, g_bwd)]
        log_fs, ks = [], []
        for rev in range(2):
            log_lb, log_1m_lb, one_m_lb = gates[rev]
            z = (zb_ref if rev else zf_ref)[pl.ds(r0s[rev], grp), :]
            t = jnp.exp(-jnp.abs(z))
            one_p_t = 1.0 + t
            log_sig = jnp.minimum(z, 0.0) - jnp.log(one_p_t)
            u = log_1m_lb + log_sig
            log_fs.append(jnp.maximum(log_lb, u) + jnp.log(1.0 + jnp.exp(-jnp.abs(log_lb - u))))
            inv = 1.0 / one_p_t
            ks.append(one_m_lb * jnp.where(z > 0.0, t * inv, inv))
        bs = [_cumsum_rows(log_fs[rev], tris[rev]) for rev in range(2)]

        q_intra, k_intra, q_in, k_out, vs, decays = [], [], [], [], [], []
        for rev in range(2):
            b, k = bs[rev], ks[rev]
            q = q_ref[pl.ds(r0s[rev], grp), :]
            mid_row = CHUNK // 2 if rev else CHUNK // 2 - 1
            last_row = 0 if rev else CHUNK - 1
            mids = [b[c * CHUNK + mid_row:c * CHUNK + mid_row + 1] for c in range(GLA_UNROLL)]
            lasts = [b[c * CHUNK + last_row:c * CHUNK + last_row + 1] for c in range(GLA_UNROLL)]
            b_mid = jnp.concatenate([jnp.broadcast_to(m, (CHUNK, m.shape[1])) for m in mids], axis=0)
            b_last = jnp.concatenate([jnp.broadcast_to(m, (CHUNK, m.shape[1])) for m in lasts], axis=0)
            q_intra.append((q * jnp.exp(b - b_mid)).astype(BF16))
            k_intra.append((k * jnp.exp(b_mid - b)).astype(BF16))
            q_in.append((q * jnp.exp(b)).astype(BF16))
            k_out.append((k * jnp.exp(b_last - b)).astype(BF16))
            vs.append(zi_ref[pl.ds(r0s[rev], grp), :].astype(BF16))
            decays.append([jnp.exp(m) for m in lasts])

        keys = [(rev, h) for rev in range(2) for h in range(hg)]
        scores = {(rev, h): lax.dot_general(q_intra[rev][:, heads[h]], k_intra[rev][:, heads[h]], contract_lanes,
                                            preferred_element_type=F32) for rev, h in keys}
        scores = {(rev, h): jnp.where(masks[rev], scores[rev, h], 0.0).astype(BF16) for rev, h in keys}
        o_intra = {(rev, h): jnp.dot(scores[rev, h], vs[rev][:, heads[h]], preferred_element_type=F32)
                   for rev, h in keys}
        kv = {(rev, h, c): lax.dot_general(vs[rev][chunks[c], heads[h]], k_out[rev][chunks[c], heads[h]],
                                           contract_rows, preferred_element_type=F32)
              for rev, h in keys for c in range(GLA_UNROLL)}
        st = {(rev, h): st_ref[rev, h] for rev, h in keys}
        for step in range(GLA_UNROLL):
            cs = [step, GLA_UNROLL - 1 - step]
            inter = {(rev, h): lax.dot_general(q_in[rev][chunks[cs[rev]], heads[h]], st[rev, h].astype(BF16),
                                               contract_lanes, preferred_element_type=F32) for rev, h in keys}
            for rev, h in keys:
                c = cs[rev]
                o_dst = ob_ref if rev else of_ref
                o_dst[pl.ds(r0s[rev] + c * CHUNK, CHUNK), heads[h]] = o_intra[rev, h][chunks[c]] + inter[rev, h]
                st[rev, h] = st[rev, h] * decays[rev][c][:, heads[h]] + kv[rev, h, c]
        for rev, h in keys:
            st_ref[rev, h] = st[rev, h]

    n_groups = n_chunks // GLA_UNROLL
    if n_groups == 1:
        pair(0, 0)
    else:
        def body(i, carry):
            pair(i, n_groups - 1 - i)
            return carry
        lax.fori_loop(0, n_groups, body, 0)

    gate = _silu(zg_ref[...])
    o = of_ref[...] + ob_ref[...]
    for h in range(hg):
        sl = slice(h * REC_HEAD, (h + 1) * REC_HEAD)
        r_ref[:, sl] = (_rms(o[:, sl]) * gain_ref[...] * gate[:, sl]).astype(r_ref.dtype)
    if has_state_out:
        own = s_out_ref.at[odd] if zero_odd is not None else s_out_ref
        for dr in range(2):
            for h in range(hg):
                own[dr, h] = st_ref[dr, h].T
        for other in zero_odd or ():
            s_out_ref[other] = jnp.zeros(s_out_ref.shape[1:], F32)


def _gla(z, lb_fwd, lb_bwd, g_gain, layer, *, n_seq, seq_len, row0, odd, n_odd, state_in=None, want_state=False,
         prev_r=None, prev_state=None, hg=4):
    width = z.shape[1] // 5
    n_heads = width // REC_HEAD
    w = hg * REC_HEAD
    ncb = width // w
    n_chunks = seq_len // CHUNK
    assert n_chunks % GLA_UNROLL == 0

    def z_spec(part):
        return pl.BlockSpec((seq_len, w), lambda s, g: (row0 // seq_len + s, part * ncb + g))

    in_specs = [z_spec(p) for p in range(5)]
    in_specs += [pl.BlockSpec((lb_fwd.shape[0], w), lambda s, g: (0, g)),
                 pl.BlockSpec((lb_bwd.shape[0], w), lambda s, g: (0, g)),
                 pl.BlockSpec((1, REC_HEAD), lambda s, g: (0, 0))]
    args = [z] * 5 + [lb_fwd, lb_bwd, g_gain.reshape(1, REC_HEAD)]
    if state_in is not None:
        in_specs.append(pl.BlockSpec((None, None, 2, hg, REC_HEAD, REC_HEAD), lambda s, g: (s, odd, 0, g, 0, 0)))
        args.append(state_in)
    out_shape = [jax.ShapeDtypeStruct((z.shape[0], width), BF16)]
    out_specs = [pl.BlockSpec((seq_len, w), lambda s, g: (row0 // seq_len + s, g))]
    zero_odd = None
    if want_state:
        out_shape.append(jax.ShapeDtypeStruct((n_seq, n_odd, 2, n_heads, REC_HEAD, REC_HEAD), F32))
        if prev_state is None:
            zero_odd = tuple(o for o in range(n_odd) if o != odd)
            out_specs.append(pl.BlockSpec((None, n_odd, 2, hg, REC_HEAD, REC_HEAD), lambda s, g: (s, 0, 0, g, 0, 0)))
        else:
            out_specs.append(pl.BlockSpec((None, None, 2, hg, REC_HEAD, REC_HEAD),
                                          lambda s, g: (s, odd, 0, g, 0, 0)))
    aliases = {}
    for out_idx, prev in enumerate((prev_r, prev_state)):
        if prev is not None:
            aliases[len(args)] = out_idx
            in_specs.append(pl.BlockSpec(memory_space=pl.ANY))
            args.append(prev)
    return pl.pallas_call(
        functools.partial(_gla_kernel, layer=layer, n_chunks=n_chunks, hg=hg,
                          has_state_in=state_in is not None, has_state_out=want_state, n_prev=len(aliases),
                          odd=odd, zero_odd=zero_odd),
        grid=(n_seq, ncb),
        in_specs=in_specs, out_specs=out_specs, out_shape=out_shape,
        input_output_aliases=aliases,
        scratch_shapes=[pltpu.VMEM((seq_len, w), F32), pltpu.VMEM((seq_len, w), F32),
                        pltpu.VMEM((2, hg, REC_HEAD, REC_HEAD), F32), pltpu.VMEM((seq_len, w), F32)],
        compiler_params=_params(("arbitrary", "arbitrary"), 48),
        name="hgrn2",
    )(*args)


def _even_mixer(lay, h, e, cache_ckv, cache_kpe, w_in_even, q_a_gain, kv_a_gain, w_qb, w_kvb, w_pool,
                pool_scale, w_out_even):
    pool_width = pool_scale.shape[1]
    q_lora, kv_lora = q_a_gain.shape[1], kv_a_gain.shape[1]
    n_heads = w_qb.shape[2] // (QK_NOPE + QK_ROPE)
    past = cache_ckv.shape[2]
    o1, o2, o3 = pool_width, pool_width + q_lora, pool_width + q_lora + kv_lora

    a = _mm([(h, 0)], w_in_even, pool_width, lead=(e,), tm=512, tn=pool_width)
    cq = _mm([(h, 0)], w_in_even[e, :, o1:o2], q_lora, tn=q_lora, out_dtype=BF16, gain=q_a_gain[e])
    ckv = _mm([(h, 0)], w_in_even[e, :, o2:o3], kv_lora, tn=kv_lora, gain=kv_a_gain[e])
    kpe = _mm([(h, 0)], w_in_even[e, :, o3:], QK_ROPE)

    wq = w_qb[e].reshape(q_lora, n_heads, QK_NOPE + QK_ROPE)
    wq = jnp.concatenate([wq[:, :, :QK_NOPE].reshape(q_lora, -1), wq[:, :, QK_NOPE:].reshape(q_lora, -1)], axis=1)
    q = _mm([(cq, 0)], wq, wq.shape[1], tn=1536)

    def with_cache(cache, new):
        lat = jnp.concatenate([cache, new[:lay.t_lat].reshape(lay.nb_lat, lay.l_lat, -1)], axis=1)
        return jnp.concatenate([lat.reshape(lay.nb_lat * (past + lay.l_lat), -1), new[lay.t_lat:]], axis=0)

    ckv_all = with_cache(cache_ckv[:, e], ckv)
    kpe_all = jnp.tile(with_cache(cache_kpe[:, e], kpe), (1, 2))
    kv = _mm([(ckv_all, 0)], w_kvb, w_kvb.shape[2], lead=(e,), tm=1664, tn=1024, out_dtype=BF16)

    lk_lat = past + lay.l_lat
    att = jnp.zeros((lay.t, n_heads * V_HEAD), BF16)
    att = _attention(q, kv, kpe_all, n_heads, n_seq=lay.nb_lat, lq=lay.l_lat, lk=lk_lat, q_row0=0, kv_row0=0,
                     tq=ROW_TILE, out_rows=lay.t, tables=_rope_tables(lay.l_lat, past), prev=att)
    att = _attention(q, kv, kpe_all, n_heads, n_seq=lay.nb_ctx, lq=lay.l_ctx, lk=lay.l_ctx, q_row0=lay.t_lat,
                     kv_row0=lay.nb_lat * lk_lat, tq=lay.l_ctx, out_rows=lay.t, prev=att)
    pooled = _pool_mixer(lay, a, w_pool, pool_scale, e)

    parts = [(pooled, 0)] + [(att, cb) for cb in range(att.shape[1] // pool_width)]
    y = _mm(parts, w_out_even, w_out_even.shape[2], lead=(e,))
    return y, ckv[lay.t_lat:], kpe[lay.t_lat:]


def _odd_mixer(lay, h, o, layer, state_hgrn, prev_states, w_in_odd, lb_fwd, lb_bwd, g_norm_gain, w_out_odd):
    n_odd = w_in_odd.shape[0]
    z = _mm([(h, 0)], w_in_odd, w_in_odd.shape[2], lead=(o,))
    r = jnp.zeros((lay.t, z.shape[1] // 5), BF16)
    (r,) = _gla(z, lb_fwd, lb_bwd, g_norm_gain[o], layer, n_seq=lay.nb_lat, seq_len=lay.l_lat, row0=0,
                odd=o, n_odd=n_odd, state_in=state_hgrn, prev_r=r)
    r, states = _gla(z, lb_fwd, lb_bwd, g_norm_gain[o], layer, n_seq=lay.nb_ctx, seq_len=lay.l_ctx,
                     row0=lay.t_lat, odd=o, n_odd=n_odd, want_state=True, prev_r=r, prev_state=prev_states)
    return _mm([(r, 0)], w_out_odd, w_out_odd.shape[2], lead=(o,)), states


def kernel(x_prompt, x_sample, cache_ckv, cache_kpe, state_hgrn, c, c_ctx, w_ada, b_ada, ffn_w1, ffn_w3, ffn_w2, w_in_even, q_a_gain, kv_a_gain, w_qb, w_kvb, w_pool, pool_scale, w_out_even, w_in_odd, lb_fwd, lb_bwd, g_norm_gain, w_out_odd, final_gain):
    nb_ctx, l_ctx, d = x_prompt.shape
    nb_lat, l_lat, _ = x_sample.shape
    depth = w_ada.shape[0]
    lay = _Layout(nb_lat, l_lat, nb_ctx, l_ctx)
    assert nb_lat + 1 <= MOD_ROWS and l_lat % ROW_TILE == 0 and l_ctx == ROW_TILE
    assert (nb_lat * (cache_ckv.shape[2] + l_lat)) % l_ctx == 0

    x = jnp.concatenate([x_sample.reshape(lay.t_lat, d), x_prompt.reshape(lay.t_ctx, d)], axis=0)
    cmat = jnp.concatenate([c, c_ctx[None, :], jnp.zeros((MOD_ROWS - nb_lat - 1, d), F32)], axis=0)
    n_mod_cols = w_ada.shape[2]
    b_ada3 = b_ada.reshape(depth, 1, n_mod_cols)
    t_ffn = _fit(lay.t, 1024)
    side_tn = _side_tile(n_mod_cols // 2, (lay.t // t_ffn) * (ffn_w1.shape[-1] // 256))

    def ffn(h, layer, half):
        if layer + 1 == depth or side_tn is None:
            return _ffn(h, ffn_w1, ffn_w3, ffn_w2, layer, half), None
        side = (cmat, w_ada, b_ada3, layer + 1, half * (n_mod_cols // 2), n_mod_cols // 2, side_tn)
        return _ffn(h, ffn_w1, ffn_w3, ffn_w2, layer, half, side=side)

    def as_mods(flat):
        return flat.reshape(MOD_ROWS, N_MOD, d)

    mods = [as_mods(_ada_mods(cmat, w_ada, b_ada3, 0))]

    new_ckv, new_kpe, new_states = [], [], None
    x, h = _transition(lay, mods, x, mod=(0, 0))
    for layer in range(depth):
        y, mods_lo = ffn(h, layer, 0)
        x, h = _transition(lay, mods, x, y, gate=(layer, 2, 0.5), mod=(layer, 3))
        if layer % 2 == 0:
            y, ckv, kpe = _even_mixer(lay, h, layer // 2, cache_ckv, cache_kpe, w_in_even, q_a_gain, kv_a_gain,
                                      w_qb, w_kvb, w_pool, pool_scale, w_out_even)
            new_ckv.append(ckv.reshape(nb_ctx, l_ctx, -1))
            new_kpe.append(kpe.reshape(nb_ctx, l_ctx, -1))
        else:
            y, new_states = _odd_mixer(lay, h, layer // 2, layer, state_hgrn, new_states, w_in_odd, lb_fwd,
                                       lb_bwd, g_norm_gain, w_out_odd)
        x, h = _transition(lay, mods, x, y, gate=(layer, 5, 1.0), mod=(layer, 6))
        y, mods_hi = ffn(h, layer, 1)
        if layer + 1 < depth:
            if mods_lo is None:
                mods.append(as_mods(_ada_mods(cmat, w_ada, b_ada3, layer + 1)))
            else:
                mods.append(as_mods(jnp.concatenate([mods_lo, mods_hi], axis=1)))
            x, h = _transition(lay, mods, x, y, gate=(layer, 8, 0.5), mod=(layer + 1, 0))
    gate = (depth - 1, 8, 0.5)
    lat_tiles = lay.t_lat // ROW_TILE
    y_sample = _transition(lay, mods, x, y, gate=gate, final_gain=final_gain, rows=(0, lat_tiles))
    y_prompt = _transition(lay, mods, x, y, gate=gate, final_gain=final_gain,
                           rows=(lat_tiles, lay.t_ctx // ROW_TILE))
    return (y_prompt.reshape(nb_ctx, l_ctx, d), y_sample.reshape(nb_lat, l_lat, d),
            jnp.stack(new_ckv, axis=1), jnp.stack(new_kpe, axis=1), new_states)
```
